```python
import math
import jax, jax.numpy as jnp
from jax import lax
import numpy as np

D_MODEL = 1024
BATCH = 2
SEQ = 8192
DEPTH = 4
DEC_BATCH = 32
DEC_SEQ = 8
PAST_LEN = 8192
PAGE_SIZE = 128

RWKV_WIDTH = D_MODEL // 2
RWKV_HEAD_SIZE = 64
RWKV_HEADS = RWKV_WIDTH // RWKV_HEAD_SIZE
W_LORA = 64
A_LORA = 64
G_LORA = 128
RWKV_PROJ = 3 * RWKV_WIDTH + W_LORA + A_LORA + G_LORA
LN_X_EPS = 64e-5
DIFF_QK_DIM = 64
DIFF_V_DIM = 2 * DIFF_QK_DIM
DIFF_HEADS = (D_MODEL // 2) // DIFF_V_DIM
DIFF_QK_WIDTH = DIFF_HEADS * 2 * DIFF_QK_DIM
DIFF_V_WIDTH = DIFF_HEADS * DIFF_V_DIM
ATTN_SCALE = DIFF_QK_DIM ** -0.5
ROPE_THETA = 10000.0
Q_BLOCK = 128
GATE_WIDTH = 2 * D_MODEL
PROJ_WIDTH = RWKV_PROJ + 2 * DIFF_QK_WIDTH + DIFF_V_WIDTH + GATE_WIDTH
D_FF = 2816
N_EXPERTS = 8
TOP_K = 2
D_FF_EXPERT = 3584
N_DENSE = (DEPTH + 1) // 2
N_MOE = DEPTH // 2
NORM_EPS = 1e-6

kernel_name = 'rwkv7_diffattn_gated_hybrid_step'


def _split(x, sizes):
    return jnp.split(x, np.cumsum(sizes)[:-1].tolist(), axis=-1)


def rms_norm(x, g, eps=NORM_EPS):
    xf = x.astype(jnp.float32)
    y = xf * lax.rsqrt(jnp.mean(xf * xf, axis=-1, keepdims=True) + eps)
    return (y * g.astype(jnp.float32)).astype(x.dtype)


def head_group_norm(y, g, b):
    B, T, H, S = y.shape
    mu = jnp.mean(y, axis=-1, keepdims=True)
    yc = y - mu
    yn = yc * lax.rsqrt(jnp.mean(yc * yc, axis=-1, keepdims=True) + LN_X_EPS)
    return yn.reshape(B, T, H * S) * g.astype(jnp.float32) + b.astype(jnp.float32)


def rope_cos_sin(pos):
    half = DIFF_QK_DIM // 2
    inv_freq = ROPE_THETA ** (-jnp.arange(half, dtype=jnp.float32) / half)
    ang = pos.astype(jnp.float32)[:, None] * inv_freq[None, :]
    return jnp.cos(ang), jnp.sin(ang)


def apply_rope(x, cos, sin):
    c = cos[None, :, None, None, :]
    s = sin[None, :, None, None, :]
    x1, x2 = jnp.split(x, 2, axis=-1)
    return jnp.concatenate([x1 * c - x2 * s, x2 * c + x1 * s], axis=-1)


def wkv_scan(S0, r, w, k, v, a_vec, b_vec):
    def step(S, inp):
        r_t, w_t, k_t, v_t, a_t, b_t = inp
        sa = jnp.einsum('bhvk,bhk->bhv', S, a_t)
        S = S * w_t[:, :, None, :] + sa[..., None] * b_t[:, :, None, :] + v_t[..., None] * k_t[:, :, None, :]
        return S, jnp.einsum('bhvk,bhk->bhv', S, r_t)
    xs = tuple(jnp.moveaxis(t, 1, 0) for t in (r, w, k, v, a_vec, b_vec))
    S, y = lax.scan(step, S0, xs)
    return jnp.moveaxis(y, 0, 1), S


def rwkv7_branch(cols, shift_prev, wkv0, lw):
    B, T, _ = cols.shape
    prev = jnp.concatenate([shift_prev[:, None, :].astype(cols.dtype), cols[:, :-1]], axis=1)
    xs = (cols + (prev - cols) * lw['mu_shift']).astype(jnp.float32)
    r, k, v, wd, ad, gd = _split(xs, [RWKV_WIDTH, RWKV_WIDTH, RWKV_WIDTH, W_LORA, A_LORA, G_LORA])
    w_log = -jax.nn.softplus(-(lw['w0'] + jnp.tanh(wd) @ lw['w_up'])) - 0.5
    decay = jnp.exp(-jnp.exp(w_log))
    a = jax.nn.sigmoid(lw['a0'] + ad @ lw['a_up'])
    g = jax.nn.sigmoid(gd) @ lw['g_up']
    hs = lambda t: t.reshape(B, T, RWKV_HEADS, RWKV_HEAD_SIZE)
    kk = hs(k * lw['k_k'])
    kk = kk / jnp.maximum(jnp.sqrt(jnp.sum(kk * kk, axis=-1, keepdims=True)), 1e-12)
    k = k * (1.0 + (a - 1.0) * lw['k_a'])
    rh, kh, vh, ah = hs(r), hs(k), hs(v), hs(a)
    y, wkv = wkv_scan(wkv0.astype(jnp.float32), rh, hs(decay), kh, vh, -kk, kk * ah)
    y = head_group_norm(y, lw['ln_x_g'], lw['ln_x_b'])
    bonus = jnp.sum(rh * kh * lw['r_k'], axis=-1, keepdims=True) * vh
    y = y + bonus.reshape(B, T, RWKV_WIDTH)
    out = (y * g).astype(cols.dtype) @ lw['w_o_rwkv']
    return out, cols[:, -1], wkv


def prompt_attention(q, k, v, lam):
    B, T = q.shape[:2]
    n_blocks = T // Q_BLOCK
    q_blocks = jnp.moveaxis(q.reshape(B, n_blocks, Q_BLOCK, DIFF_HEADS, 2, DIFF_QK_DIM), 1, 0)
    k_pos = jnp.arange(T)

    def one_block(args):
        q_blk, blk = args
        s = jnp.einsum('bqhmd,bkhmd->bhmqk', q_blk, k) * ATTN_SCALE
        q_pos = blk * Q_BLOCK + jnp.arange(Q_BLOCK)
        s = jnp.where(k_pos[None, :] <= q_pos[:, None], s, -jnp.inf)
        p = jax.nn.softmax(s, axis=-1)
        w = p[:, :, 0] - lam * p[:, :, 1]
        return jnp.einsum('bhqk,bkhv->bqhv', w, v)

    o = lax.map(one_block, (q_blocks, jnp.arange(n_blocks)))
    return jnp.moveaxis(o, 0, 1).reshape(B, T, DIFF_HEADS, DIFF_V_DIM)


def sample_attention(q, k, v, k_past, v_past, lam):
    T = q.shape[1]
    kp = k_past.astype(jnp.float32)
    vp = v_past.astype(jnp.float32)
    n_past = kp.shape[1]
    s_past = jnp.einsum('bqhmd,bkhmd->bhmqk', q, kp) * ATTN_SCALE
    s_new = jnp.einsum('bqhmd,bkhmd->bhmqk', q, k) * ATTN_SCALE
    s_new = jnp.where(jnp.tril(jnp.ones((T, T), dtype=bool)), s_new, -jnp.inf)
    p = jax.nn.softmax(jnp.concatenate([s_past, s_new], axis=-1), axis=-1)
    w = p[:, :, 0] - lam * p[:, :, 1]
    return jnp.einsum('bhqk,bkhv->bqhv', w[..., :n_past], vp) + jnp.einsum('bhqk,bkhv->bqhv', w[..., n_past:], v)


def diff_attn_branch(q, k, v, pos, k_past, v_past, lw, lam_init):
    B, T, _ = q.shape
    q = q.reshape(B, T, DIFF_HEADS, 2, DIFF_QK_DIM)
    k = k.reshape(B, T, DIFF_HEADS, 2, DIFF_QK_DIM)
    v = v.reshape(B, T, DIFF_HEADS, DIFF_V_DIM)
    cos, sin = rope_cos_sin(pos)
    q = apply_rope(rms_norm(q, lw['q_norm_g']).astype(jnp.float32), cos, sin)
    k = apply_rope(rms_norm(k, lw['k_norm_g']).astype(jnp.float32), cos, sin)
    lp = lw['lam'].astype(jnp.float32)
    lam = jnp.exp(jnp.sum(lp[0] * lp[1])) - jnp.exp(jnp.sum(lp[2] * lp[3])) + lam_init
    vf = v.astype(jnp.float32)
    if k_past is None:
        o = prompt_attention(q, k, vf, lam)
    else:
        o = sample_attention(q, k, vf, k_past, v_past, lam)
    o = rms_norm(o, lw['subln_g']) * (1.0 - lam_init)
    out = o.reshape(B, T, DIFF_V_WIDTH).astype(v.dtype) @ lw['w_o_diff']
    return out, k.astype(v.dtype), v


def swiglu(h, w1, w3, w2):
    return (jax.nn.silu(h @ w1) * (h @ w3)) @ w2


def moe_swiglu(h, router, w1, w3, w2):
    logits = (h @ router).astype(jnp.float32)
    top_val, top_idx = lax.top_k(logits, TOP_K)
    top_gate = jax.nn.softmax(top_val, axis=-1)
    gate = jnp.sum(jax.nn.one_hot(top_idx, N_EXPERTS, dtype=jnp.float32) * top_gate[..., None], axis=-2)
    out = jnp.zeros(h.shape[:-1] + (D_MODEL,), h.dtype)
    for e in range(N_EXPERTS):
        out = out + gate[..., e:e + 1].astype(h.dtype) * swiglu(h, w1[e], w3[e], w2[e])
    return out


def trunk_layer(x, c, pos, shift_prev, wkv0, k_past, v_past, lw, l):
    mod = c @ lw['w_ada'] + lw['b_ada']
    sh1, sc1, g1, sh2, sc2, g2 = [m[:, None, :] for m in jnp.split(mod, 6, axis=-1)]
    h = rms_norm(x, lw['norm1_g']) * (1.0 + sc1) + sh1
    proj = h @ lw['w_in']
    a_cols, q, k, v, gates = _split(proj, [RWKV_PROJ, DIFF_QK_WIDTH, DIFF_QK_WIDTH, DIFF_V_WIDTH, GATE_WIDTH])
    out_a, new_shift, new_wkv = rwkv7_branch(a_cols, shift_prev, wkv0, lw)
    lam_init = 0.8 - 0.6 * math.exp(-0.3 * l)
    out_b, k_rows, v_rows = diff_attn_branch(q, k, v, pos, k_past, v_past, lw, lam_init)
    g_a, g_b = jnp.split(gates, 2, axis=-1)
    merged = jax.nn.sigmoid(g_a) * out_a + jax.nn.sigmoid(g_b) * out_b
    x = x + g1 * (merged @ lw['w_out'])
    h2 = rms_norm(x, lw['norm2_g']) * (1.0 + sc2) + sh2
    if l % 2 == 0:
        f = swiglu(h2, *lw['ffn'])
    else:
        f = moe_swiglu(h2, *lw['moe'])
    x = x + g2 * f
    return x, k_rows, v_rows, new_wkv, new_shift


def setup_inputs(seed: int = 0) -> dict:
    key = jax.random.key(seed)
    keys = iter(jax.random.split(key, 64))

    def normal(shape, scale=1.0):
        return jax.random.normal(next(keys), shape, jnp.float32) * scale

    def unif(shape, lo, hi):
        return jax.random.uniform(next(keys), shape, jnp.float32, lo, hi)

    n_pages = PAST_LEN // PAGE_SIZE
    n_used = DEC_BATCH * n_pages
    n_phys = n_used + max(1, n_used // 4)
    page_table = jax.random.permutation(next(keys), n_phys)[:n_used].reshape(DEC_BATCH, n_pages).astype(jnp.int32)
    D = D_MODEL
    return {
        'x_prompt': normal((BATCH, SEQ, D)),
        'x_sample': normal((DEC_BATCH, DEC_SEQ, D)),
        'c_prompt': normal((BATCH, D)),
        'c_sample': normal((DEC_BATCH, D)),
        'cache_k': normal((DEPTH, n_phys, PAGE_SIZE, DIFF_HEADS, 2, DIFF_QK_DIM)),
        'cache_v': normal((DEPTH, n_phys, PAGE_SIZE, DIFF_HEADS, DIFF_V_DIM)),
        'state_wkv': normal((DEPTH, DEC_BATCH, RWKV_HEADS, RWKV_HEAD_SIZE, RWKV_HEAD_SIZE), 0.5),
        'state_shift': normal((DEPTH, DEC_BATCH, RWKV_PROJ)),
        'page_table': page_table,
        'norm1_g': 1.0 + normal((DEPTH, D), 0.02),
        'norm2_g': 1.0 + normal((DEPTH, D), 0.02),
        'w_ada': normal((DEPTH, D, 6 * D), 0.2 * D ** -0.5),
        'b_ada': normal((DEPTH, 6 * D), 0.02),
        'w_in': normal((DEPTH, D, PROJ_WIDTH), D ** -0.5),
        'mu_shift': unif((DEPTH, RWKV_PROJ), 0.0, 1.0),
        'w0': unif((DEPTH, RWKV_WIDTH), -3.0, 1.0),
        'w_up': normal((DEPTH, W_LORA, RWKV_WIDTH), 0.1),
        'a0': normal((DEPTH, RWKV_WIDTH), 0.1),
        'a_up': normal((DEPTH, A_LORA, RWKV_WIDTH), 0.5 * A_LORA ** -0.5),
        'g_up': normal((DEPTH, G_LORA, RWKV_WIDTH), G_LORA ** -0.5),
        'k_k': 0.85 + normal((DEPTH, RWKV_WIDTH), 0.02),
        'k_a': 1.0 + normal((DEPTH, RWKV_WIDTH), 0.02),
        'r_k': normal((DEPTH, RWKV_HEADS, RWKV_HEAD_SIZE), 0.1),
        'ln_x_g': 1.0 + normal((DEPTH, RWKV_WIDTH), 0.02),
        'ln_x_b': normal((DEPTH, RWKV_WIDTH), 0.02),
        'w_o_rwkv': normal((DEPTH, RWKV_WIDTH, D), RWKV_WIDTH ** -0.5),
        'q_norm_g': 1.0 + normal((DEPTH, 2, DIFF_QK_DIM), 0.02),
        'k_norm_g': 1.0 + normal((DEPTH, 2, DIFF_QK_DIM), 0.02),
        'lam': normal((DEPTH, 4, DIFF_QK_DIM), 0.1),
        'subln_g': 1.0 + normal((DEPTH, DIFF_V_DIM), 0.02),
        'w_o_diff': normal((DEPTH, DIFF_V_WIDTH, D), DIFF_V_WIDTH ** -0.5),
        'w_out': normal((DEPTH, D, D), D ** -0.5),
        'ffn_w1': normal((N_DENSE, D, D_FF), D ** -0.5),
        'ffn_w3': normal((N_DENSE, D, D_FF), D ** -0.5),
        'ffn_w2': normal((N_DENSE, D_FF, D), D_FF ** -0.5),
        'router': normal((N_MOE, D, N_EXPERTS), D ** -0.5),
        'moe_w1': normal((N_MOE, N_EXPERTS, D, D_FF_EXPERT), D ** -0.5),
        'moe_w3': normal((N_MOE, N_EXPERTS, D, D_FF_EXPERT), D ** -0.5),
        'moe_w2': normal((N_MOE, N_EXPERTS, D_FF_EXPERT, D), D_FF_EXPERT ** -0.5),
    }


def reference(x_prompt, x_sample, c_prompt, c_sample, cache_k, cache_v, state_wkv, state_shift, page_table,
              norm1_g, norm2_g, w_ada, b_ada, w_in, mu_shift, w0, w_up, a0, a_up, g_up, k_k, k_a, r_k,
              ln_x_g, ln_x_b, w_o_rwkv, q_norm_g, k_norm_g, lam, subln_g, w_o_diff, w_out,
              ffn_w1, ffn_w3, ffn_w2, router, moe_w1, moe_w3, moe_w2):
    n_batch = x_prompt.shape[0]
    n_dec = x_sample.shape[0]
    past_len = page_table.shape[1] * cache_k.shape[2]
    pos_prompt = jnp.arange(x_prompt.shape[1])
    pos_sample = past_len + jnp.arange(x_sample.shape[1])
    shift0 = jnp.zeros((n_batch, RWKV_PROJ), x_prompt.dtype)
    wkv_zero = jnp.zeros((n_batch, RWKV_HEADS, RWKV_HEAD_SIZE, RWKV_HEAD_SIZE), jnp.float32)

    xp, xs = x_prompt, x_sample
    kp_l, vp_l, ks_l, vs_l, wp_l, ws_l, sp_l, ss_l = [], [], [], [], [], [], [], []
    for l in range(DEPTH):
        lw = {
            'norm1_g': norm1_g[l], 'norm2_g': norm2_g[l], 'w_ada': w_ada[l], 'b_ada': b_ada[l],
            'w_in': w_in[l], 'mu_shift': mu_shift[l], 'w0': w0[l], 'w_up': w_up[l], 'a0': a0[l],
            'a_up': a_up[l], 'g_up': g_up[l], 'k_k': k_k[l], 'k_a': k_a[l], 'r_k': r_k[l],
            'ln_x_g': ln_x_g[l], 'ln_x_b': ln_x_b[l], 'w_o_rwkv': w_o_rwkv[l],
            'q_norm_g': q_norm_g[l], 'k_norm_g': k_norm_g[l], 'lam': lam[l], 'subln_g': subln_g[l],
            'w_o_diff': w_o_diff[l], 'w_out': w_out[l],
        }
        if l % 2 == 0:
            i = l // 2
            lw['ffn'] = (ffn_w1[i], ffn_w3[i], ffn_w2[i])
        else:
            i = l // 2
            lw['moe'] = (router[i], moe_w1[i], moe_w3[i], moe_w2[i])

        xp, k_rows, v_rows, wkv_p, shift_p = trunk_layer(xp, c_prompt, pos_prompt, shift0, wkv_zero, None, None, lw, l)
        kp_l.append(k_rows); vp_l.append(v_rows); wp_l.append(wkv_p); sp_l.append(shift_p)

        k_past = cache_k[l][page_table].reshape(n_dec, past_len, DIFF_HEADS, 2, DIFF_QK_DIM)
        v_past = cache_v[l][page_table].reshape(n_dec, past_len, DIFF_HEADS, DIFF_V_DIM)
        xs, k_rows, v_rows, wkv_s, shift_s = trunk_layer(xs, c_sample, pos_sample, state_shift[l], state_wkv[l],
                                                          k_past, v_past, lw, l)
        ks_l.append(k_rows); vs_l.append(v_rows); ws_l.append(wkv_s); ss_l.append(shift_s)

    y_prompt = xp
    y_sample = xs
    k_prompt = jnp.stack(kp_l)
    v_prompt = jnp.stack(vp_l)
    k_sample = jnp.stack(ks_l)
    v_sample = jnp.stack(vs_l)
    wkv_prompt = jnp.stack(wp_l)
    wkv_sample = jnp.stack(ws_l)
    shift_prompt = jnp.stack(sp_l)
    shift_sample = jnp.stack(ss_l)
    return (y_prompt, y_sample, k_prompt, v_prompt, k_sample, v_sample, wkv_prompt, wkv_sample, shift_prompt, shift_sample)
```

```python
import functools
import math

import jax
import jax.numpy as jnp
from jax import lax
from jax.experimental import pallas as pl
from jax.experimental.pallas import tpu as pltpu

F32 = jnp.float32
BF16 = jnp.bfloat16

NORM_EPS = 1e-6
LN_X_EPS = 64e-5
ROPE_THETA = 10000.0
TOP_K = 2
KK_NORM_FLOOR = 1e-12
LANES = 128
VMEM_LIMIT = 56 * 1024 * 1024


def _params(*sem):
    return pltpu.CompilerParams(dimension_semantics=sem, vmem_limit_bytes=VMEM_LIMIT)


def _tile(n, pref):
    if n <= pref:
        return n
    t = pref
    while n % t:
        t //= 2
    return t


def _bdot(a, b):
    return jnp.dot(a.astype(BF16), b.astype(BF16), preferred_element_type=F32)


def _mm(a, b, precise):
    return _dot3(a.astype(F32), b.astype(F32)) if precise else _bdot(a, b)


def _dot_nt(a, b):
    return lax.dot_general(a, b, (((1,), (1,)), ((), ())), preferred_element_type=F32)


def _split2(x):
    hi = x.astype(BF16)
    lo = (x - hi.astype(F32)).astype(BF16)
    return hi, lo


def _split3(x):
    hi = x.astype(BF16)
    r1 = x - hi.astype(F32)
    mid = r1.astype(BF16)
    lo = (r1 - mid.astype(F32)).astype(BF16)
    return hi, mid, lo


def _dot3(a, b):
    ah, al = _split2(a)
    bh, bl = _split2(b)
    d = functools.partial(jnp.dot, preferred_element_type=F32)
    return d(ah, bh) + d(ah, bl) + d(al, bh)


def _dot3_nt(a, b):
    ah, al = _split2(a)
    bh, bl = _split2(b)
    return _dot_nt(ah, bh) + _dot_nt(ah, bl) + _dot_nt(al, bh)


def _dot_exact_rhs(x, m_bf):
    hi, mid, lo = _split3(x)
    d = functools.partial(jnp.dot, preferred_element_type=F32)
    return d(hi, m_bf) + d(mid, m_bf) + d(lo, m_bf)


def _dot_exact_lhs(m_bf, x):
    hi, mid, lo = _split3(x)
    d = functools.partial(jnp.dot, preferred_element_type=F32)
    return d(m_bf, hi) + d(m_bf, mid) + d(m_bf, lo)


def _sigmoid(x):
    return 1.0 / (1.0 + jnp.exp(-x))


def _softplus(x):
    return jnp.maximum(x, 0.0) + jnp.log(1.0 + jnp.exp(-jnp.abs(x)))


def _iota(shape, dim):
    return lax.broadcasted_iota(jnp.int32, shape, dim)


def _ada_kernel(c_ref, w_ref, b_ref, o_ref):
    o_ref[0] = _dot3(c_ref[...], w_ref[0]) + b_ref[0]


def _ada_call(c_all, w_ada, b_ada):
    n_layers, d, d6 = w_ada.shape
    m = c_all.shape[0]
    tn = _tile(d6, 1536)
    return pl.pallas_call(
        _ada_kernel,
        grid=(n_layers, d6 // tn),
        in_specs=[
            pl.BlockSpec((m, d), lambda l, j: (0, 0)),
            pl.BlockSpec((1, d, tn), lambda l, j: (l, 0, j)),
            pl.BlockSpec((1, 1, tn), lambda l, j: (l, 0, j)),
        ],
        out_specs=pl.BlockSpec((1, m, tn), lambda l, j: (l, 0, j)),
        out_shape=jax.ShapeDtypeStruct((n_layers, m, d6), F32),
        compiler_params=_params("parallel", "parallel"),
        name="adaln_mod",
    )(c_all, w_ada, b_ada.reshape(n_layers, 1, d6))


class _Rows:
    def __init__(self, n, seq_len, mod3):
        self.n, self.seq_len, self.mod3 = n, seq_len, mod3
        self.per_row = mod3.shape[1] != 1

    def mod_spec(self, tm, d, col, n_grid):
        r = self.mod3.shape[1]
        if self.per_row:
            assert r == tm == self.n
            idx = lambda i, *_: (0, 0, col)
        else:
            assert self.seq_len % tm == 0
            per = self.seq_len // tm
            idx = lambda i, *_: (i // per, 0, col)
        return pl.BlockSpec((1, r, d), idx)


def _proj_kernel(x_ref, sh_ref, sc_ref, g_ref, w_ref, o_ref, h_scr, *, precise):
    @pl.when(pl.program_id(1) == 0)
    def _():
        x = x_ref[...]
        y = x * lax.rsqrt(jnp.mean(x * x, axis=-1, keepdims=True) + NORM_EPS)
        h = (y * g_ref[...]) * (1.0 + sc_ref[0]) + sh_ref[0]
        h_scr[...] = h.astype(h_scr.dtype)

    o_ref[...] = _mm(h_scr[...], w_ref[...], precise)


def _proj_call(x, rows, norm_g, w, tn):
    n, d = x.shape
    pw = w.shape[1]
    tm = _tile(n, 512)
    precise = w.dtype == F32
    return pl.pallas_call(
        functools.partial(_proj_kernel, precise=precise),
        grid=(n // tm, pw // tn),
        in_specs=[
            pl.BlockSpec((tm, d), lambda i, j: (i, 0)),
            rows.mod_spec(tm, d, 0, 2),
            rows.mod_spec(tm, d, 1, 2),
            pl.BlockSpec((1, d), lambda i, j: (0, 0)),
            pl.BlockSpec((d, tn), lambda i, j: (0, j)),
        ],
        out_specs=pl.BlockSpec((tm, tn), lambda i, j: (i, j)),
        out_shape=jax.ShapeDtypeStruct((n, pw), F32),
        scratch_shapes=[pltpu.VMEM((tm, d), w.dtype)],
        compiler_params=_params("parallel", "arbitrary"),
        name="proj_in",
    )(x, rows.mod3, rows.mod3, norm_g, w)


def _rwkv_prep_kernel(cols_ref, prev8_ref, init_ref, mu_ref, w0_ref, wup_ref, a0_ref, aup_ref,
                      gup_ref, kk_ref, ka_ref, ones_ref,
                      r_ref, ld_ref, k_ref, v_ref, av_ref, bv_ref, g_ref, *, seq_len, tm, width, precise):
    cols = cols_ref[...]
    rolled = pltpu.roll(cols, 1, 0)
    row = _iota((tm, 1), 0)
    if seq_len >= tm:
        first_tile = (pl.program_id(0) % (seq_len // tm)) == 0
        boundary = jnp.where(first_tile, init_ref[0], prev8_ref[7:8, :])
        prev = jnp.where(row == 0, boundary, rolled)
    else:
        prev = jnp.where(row % seq_len == 0, init_ref[...], rolled)
    xs = cols + (prev - cols) * mu_ref[...]
    w = width
    r = xs[:, 0:w]
    k = xs[:, w:2 * w]
    v = xs[:, 2 * w:3 * w]
    lora_wa = xs[:, 3 * w:3 * w + LANES]
    gd = xs[:, 3 * w + LANES:]
    w_log = -_softplus(-(w0_ref[...] + _mm(jnp.tanh(lora_wa), wup_ref[...], precise))) - 0.5
    a = _sigmoid(a0_ref[...] + _mm(lora_wa, aup_ref[...], precise))
    g = _mm(_sigmoid(gd), gup_ref[...], precise)
    kk = k * kk_ref[...]
    ss = _dot_exact_rhs(kk * kk, ones_ref[...])
    kk = kk / jnp.maximum(jnp.sqrt(ss), KK_NORM_FLOOR)
    r_ref[...] = r
    ld_ref[...] = -jnp.exp(w_log)
    k_ref[...] = k * (1.0 + (a - 1.0) * ka_ref[...])
    v_ref[...] = v
    av_ref[...] = -kk
    bv_ref[...] = kk * a
    g_ref[...] = g


def _rwkv_prep_call(proj, rows, init_rows, lw, col_block, ones_head, precise):
    n = proj.shape[0]
    width = lw["w0"].shape[-1]
    pw = lw["mu_shift"].shape[-1]
    tm = _tile(n, 256) if rows.seq_len >= 256 else n
    assert pw == 3 * width + LANES + lw["g_up"].shape[0]
    if rows.seq_len >= tm:
        init_spec = pl.BlockSpec((1, 1, pw), lambda i: ((i * tm) // rows.seq_len, 0, 0))
    else:
        init_spec = pl.BlockSpec((tm, pw), lambda i: (0, 0))
    row_spec = lambda wd: pl.BlockSpec((1, wd), lambda i: (0, 0))
    full = lambda a: pl.BlockSpec(a.shape, lambda i: (0,) * a.ndim)
    out = jax.ShapeDtypeStruct((n, width), F32)
    kern = functools.partial(_rwkv_prep_kernel, seq_len=rows.seq_len, tm=tm, width=width, precise=precise)
    return pl.pallas_call(
        kern,
        grid=(n // tm,),
        in_specs=[
            pl.BlockSpec((tm, pw), lambda i: (i, col_block)),
            pl.BlockSpec((8, pw), lambda i: (jnp.maximum(i * (tm // 8) - 1, 0), col_block)),
            init_spec,
            row_spec(pw), row_spec(width), full(lw["w_up_pad"]), row_spec(width), full(lw["a_up_pad"]),
            full(lw["g_up"]), row_spec(width), row_spec(width), full(ones_head),
        ],
        out_specs=[pl.BlockSpec((tm, width), lambda i: (i, 0))] * 7,
        out_shape=[out] * 7,
        compiler_params=_params("parallel"),
        name="rwkv_prep",
    )(proj, proj, init_rows, lw["mu_shift"], lw["w0"], lw["w_up_pad"], lw["a0"], lw["a_up_pad"],
      lw["g_up"], lw["k_k"], lw["k_a"], ones_head)


def _wkv_kernel(r_ref, ld_ref, k_ref, v_ref, a_ref, b_ref, s0_ref, y_ref, s_ref, *, chunk, heads, hs):
    c = chunk

    @pl.when(pl.program_id(1) == 0)
    def _():
        s_ref[...] = s0_ref[...]

    rr = _iota((c, c), 0)
    cc = _iota((c, c), 1)
    strict = rr > cc
    incl = rr >= cc
    tri = incl.astype(BF16)
    eye_c = (rr == cc).astype(F32)
    eye_v = (_iota((hs, hs), 0) == _iota((hs, hs), 1)).astype(BF16)

    ld = ld_ref[...]
    cum = _dot_exact_lhs(tri, ld)
    tot = cum[c - 1:c, :]
    e_neg = jnp.exp(-cum)
    e_dec = jnp.exp(tot - cum)
    a_t = a_ref[...] * jnp.exp(cum - ld)
    r_t = r_ref[...] * jnp.exp(cum)
    b_t = b_ref[...] * e_neg
    k_t = k_ref[...] * e_neg
    b_d = b_ref[...] * e_dec
    k_d = k_ref[...] * e_dec
    e_tot = jnp.exp(tot)
    v_all = v_ref[...]

    n_double = int(math.log2(c)) - 1
    for h in range(heads):
        sl = slice(h * hs, (h + 1) * hs)
        ar = jnp.concatenate([a_t[:, sl], r_t[:, sl]], axis=0)
        bk = jnp.concatenate([b_t[:, sl], k_t[:, sl]], axis=0)
        pm = _dot3_nt(ar, bk)
        l_ab = jnp.where(strict, pm[:c, :c], 0.0)
        l_ak = jnp.where(strict, pm[:c, c:], 0.0)
        m_rb = jnp.where(incl, pm[c:, :c], 0.0)
        m_rk = jnp.where(incl, pm[c:, c:], 0.0)
        t_inv = eye_c + l_ab
        l_pow = l_ab
        for _ in range(n_double):
            l_pow = _dot3(l_pow, l_pow)
            t_inv = t_inv + _dot3(t_inv, l_pow)
        vh = v_all[:, sl]
        z = _dot3(l_ak, vh)
        au = _dot3(t_inv, jnp.concatenate([a_t[:, sl], z], axis=1))
        s0 = s_ref[0, h]
        u = _dot3_nt(au[:, :hs], s0) + au[:, hs:]
        y = _dot3_nt(r_t[:, sl], s0) + _dot3(m_rb, u) + _dot3(m_rk, vh)
        uv = jnp.concatenate([u, vh], axis=0)
        uh, um, ul = _split3(uv)
        uv_t = _dot_nt(eye_v, uh) + _dot_nt(eye_v, um) + _dot_nt(eye_v, ul)
        bkd = jnp.concatenate([b_d[:, sl], k_d[:, sl]], axis=0)
        s_ref[0, h] = s0 * e_tot[:, sl] + _dot3(uv_t, bkd)
        y_ref[:, sl] = y


def _wkv_call(r, ld, k, v, av, bv, state0, seq_len, chunk):
    n, width = r.shape
    nb, heads, hs, _ = state0.shape
    n_chunks = seq_len // chunk
    tok = pl.BlockSpec((chunk, width), lambda b, c: (b * n_chunks + c, 0))
    st = pl.BlockSpec((1, heads, hs, hs), lambda b, c: (b, 0, 0, 0))
    kern = functools.partial(_wkv_kernel, chunk=chunk, heads=heads, hs=hs)
    return pl.pallas_call(
        kern,
        grid=(nb, n_chunks),
        in_specs=[tok] * 6 + [st],
        out_specs=[tok, st],
        out_shape=[jax.ShapeDtypeStruct((n, width), F32), jax.ShapeDtypeStruct(state0.shape, F32)],
        compiler_params=_params("parallel", "arbitrary"),
        name="wkv_scan",
    )(r, ld, k, v, av, bv, state0)


def _attn_prep_kernel(q_ref, k_ref, v_ref, cos_ref, sin_ref, qg_ref, kg_ref, ones_ref,
                      kr_ref, vr_ref, qo_ref, kb_ref, vb_ref, *, qk_dim, scale):
    ones = ones_ref[...]
    cos = cos_ref[...]
    sin = sin_ref[...]
    lane = _iota(cos.shape, 1)
    first_half = (lane % qk_dim) < (qk_dim // 2)
    width = cos.shape[1]
    half = qk_dim // 2

    def norm_rope(x, g):
        ms = _dot_exact_rhs(x * x, ones) * (1.0 / qk_dim)
        y = x * lax.rsqrt(ms + NORM_EPS) * g
        swapped = jnp.where(first_half, pltpu.roll(y, width - half, 1), pltpu.roll(y, half, 1))
        return y * cos + swapped * sin

    q = norm_rope(q_ref[...], qg_ref[...])
    k = norm_rope(k_ref[...], kg_ref[...])
    v = v_ref[...]
    kr_ref[...] = k
    vr_ref[...] = v
    qo_ref[...] = (q * scale).astype(qo_ref.dtype)
    kb_ref[...] = k.astype(BF16)
    vb_ref[...] = v.astype(BF16)


def _attn_prep_call(proj, rows, cos, sin, qg, kg, ones_qk, col_blocks, q_dtype, qk_dim):
    n = proj.shape[0]
    width = qg.shape[-1]
    tm = _tile(n, 512) if rows.seq_len >= 512 else n
    per = max(rows.seq_len // tm, 1)
    tab = pl.BlockSpec((tm, width), lambda i: (i % per, 0))
    col = lambda cb: pl.BlockSpec((tm, width), lambda i: (i, cb))
    row_spec = pl.BlockSpec((1, width), lambda i: (0, 0))
    out_spec = pl.BlockSpec((tm, width), lambda i: (i, 0))
    kern = functools.partial(_attn_prep_kernel, qk_dim=qk_dim, scale=qk_dim ** -0.5)
    sds = lambda dt: jax.ShapeDtypeStruct((n, width), dt)
    return pl.pallas_call(
        kern,
        grid=(n // tm,),
        in_specs=[col(col_blocks[0]), col(col_blocks[1]), col(col_blocks[2]), tab, tab, row_spec, row_spec,
                  pl.BlockSpec(ones_qk.shape, lambda i: (0, 0))],
        out_specs=[out_spec] * 5,
        out_shape=[sds(F32), sds(F32), sds(q_dtype), sds(BF16), sds(BF16)],
        compiler_params=_params("parallel"),
        name="attn_prep",
    )(proj, proj, proj, cos, sin, qg, kg, ones_qk)


def _lambda(lam_ref, lam_init):
    lp = lam_ref[...]
    s1 = jnp.sum(lp[0:1] * lp[1:2], axis=-1, keepdims=True)
    s2 = jnp.sum(lp[2:3] * lp[3:4], axis=-1, keepdims=True)
    return jnp.exp(s1) - jnp.exp(s2) + lam_init


def _stack_maps(qh, qk_dim):
    lane = _iota(qh.shape, 1)
    zero = jnp.zeros_like(qh)
    return jnp.concatenate([jnp.where(lane < qk_dim, qh, zero), jnp.where(lane >= qk_dim, qh, zero)], axis=0)


def _flash_kernel(lam_ref, q_ref, k_ref, v_ref, o_ref, qs_scr, m_scr, l_scr, acc_scr, *, tq, qk_dim, lam_init):
    i = pl.program_id(2)
    qs_scr[...] = _stack_maps(q_ref[...], qk_dim)
    m_scr[...] = jnp.full(m_scr.shape, -jnp.inf, F32)
    l_scr[...] = jnp.zeros(l_scr.shape, F32)
    acc_scr[...] = jnp.zeros(acc_scr.shape, F32)

    def step(j, masked):
        start = pl.multiple_of(j * tq, tq)
        kc = k_ref[pl.ds(start, tq), :]
        vc = v_ref[pl.ds(start, tq), :]
        s = _dot_nt(qs_scr[...], kc)
        if masked:
            qpos = _iota((tq, tq), 0)
            kpos = _iota((tq, tq), 1)
            keep = kpos <= qpos
            keep = jnp.concatenate([keep, keep], axis=0)
            s = jnp.where(keep, s, -jnp.inf)
        m_prev = m_scr[...]
        m_new = jnp.maximum(m_prev, jnp.max(s, axis=-1, keepdims=True))
        alpha = jnp.exp(m_prev - m_new)
        p = jnp.exp(s - m_new)
        l_scr[...] = alpha * l_scr[...] + jnp.sum(p, axis=-1, keepdims=True)
        acc_scr[...] = alpha * acc_scr[...] + jnp.dot(p.astype(BF16), vc, preferred_element_type=F32)
        m_scr[...] = m_new

    def body(j, carry):
        step(j, False)
        return carry

    lax.fori_loop(0, i, body, 0)
    step(i, True)
    o = acc_scr[...] / l_scr[...]
    lam = _lambda(lam_ref, lam_init)
    o_ref[...] = o[:tq] - lam * o[tq:]


def _flash_call(q_bf, k_bf, v_bf, lam, n_batch, seq_len, heads, qk_dim, lam_init):
    n, width = q_bf.shape
    hw = width // heads
    tq = _tile(seq_len, 256)
    nq = seq_len // tq
    kern = functools.partial(_flash_kernel, tq=tq, qk_dim=qk_dim, lam_init=lam_init)
    qspec = pl.BlockSpec((tq, hw), lambda b, h, i: (b * nq + i, h))
    kvspec = pl.BlockSpec((seq_len, hw), lambda b, h, i: (b, h))
    return pl.pallas_call(
        kern,
        grid=(n_batch, heads, nq),
        in_specs=[pl.BlockSpec(lam.shape, lambda b, h, i: (0, 0)), qspec, kvspec, kvspec],
        out_specs=qspec,
        out_shape=jax.ShapeDtypeStruct((n, width), F32),
        scratch_shapes=[pltpu.VMEM((2 * tq, hw), BF16), pltpu.VMEM((2 * tq, 1), F32),
                        pltpu.VMEM((2 * tq, 1), F32), pltpu.VMEM((2 * tq, hw), F32)],
        compiler_params=_params("parallel", "parallel", "arbitrary"),
        name="prompt_attention",
    )(lam, q_bf, k_bf, v_bf)


def _paged_kernel(pt_ref, lam_ref, q_ref, kn_ref, vn_ref, *refs, pages, heads, qk_dim, lam_init, n_groups):
    del pt_ref
    k_refs = refs[:pages]
    v_refs = refs[pages:2 * pages]
    o_ref = refs[2 * pages]
    m_scr, l_scr, acc_scr = refs[2 * pages + 1:]
    g = pl.program_id(1)
    hw = 2 * qk_dim
    ts = q_ref.shape[0]

    @pl.when(g == 0)
    def _():
        m_scr[...] = jnp.full(m_scr.shape, -jnp.inf, F32)
        l_scr[...] = jnp.zeros(l_scr.shape, F32)
        acc_scr[...] = jnp.zeros(acc_scr.shape, F32)

    q = q_ref[...]

    def update(h, s, pv_fn):
        m_prev = m_scr[h]
        m_new = jnp.maximum(m_prev, jnp.max(s, axis=-1, keepdims=True))
        alpha = jnp.exp(m_prev - m_new)
        p = jnp.exp(s - m_new)
        l_scr[h] = alpha * l_scr[h] + jnp.sum(p, axis=-1, keepdims=True)
        acc_scr[h] = alpha * acc_scr[h] + pv_fn(p)
        m_scr[h] = m_new

    def scores(qs, keys):
        qh, ql = _split2(qs)
        kh, kl = _split2(keys)
        both = _dot_nt(jnp.concatenate([qh, ql], axis=0), kh)
        return both[:2 * ts] + both[2 * ts:] + _dot_nt(qh, kl)

    def weighted(p, vals):
        ph, plo = _split2(p)
        vh, vl = _split2(vals)
        both = jnp.dot(jnp.concatenate([ph, plo], axis=0), vh, preferred_element_type=F32)
        return both[:2 * ts] + both[2 * ts:] + jnp.dot(ph, vl, preferred_element_type=F32)

    for h in range(heads):
        sl = slice(h * hw, (h + 1) * hw)
        qs = _stack_maps(q[:, sl], qk_dim)
        s = jnp.concatenate([scores(qs, k_refs[p][0, 0, :, sl]) for p in range(pages)], axis=1)
        page = k_refs[0].shape[2]

        def pv(p_all, sl=sl, page=page):
            acc = None
            for p in range(pages):
                t = weighted(p_all[:, p * page:(p + 1) * page], v_refs[p][0, 0, :, sl])
                acc = t if acc is None else acc + t
            return acc

        update(h, s, pv)

    @pl.when(g == n_groups - 1)
    def _():
        lam = _lambda(lam_ref, lam_init)
        pad = jnp.zeros((LANES - ts, kn_ref.shape[1]), F32)
        kn = jnp.concatenate([kn_ref[...], pad], axis=0)
        vn = jnp.concatenate([vn_ref[...], pad], axis=0)
        qpos = _iota((2 * ts, LANES), 0) % ts
        kpos = _iota((2 * ts, LANES), 1)
        keep = kpos <= qpos
        for h in range(heads):
            sl = slice(h * hw, (h + 1) * hw)
            qs = _stack_maps(q[:, sl], qk_dim)
            s = jnp.where(keep, scores(qs, kn[:, sl]), -jnp.inf)
            update(h, s, lambda p, sl=sl: weighted(p, vn[:, sl]))
            o = acc_scr[h] / l_scr[h]
            o_ref[:, sl] = o[:ts] - lam * o[ts:]


def _paged_call(page_table, lam, q, k_new, v_new, cache_k, cache_v, layer, heads, qk_dim, lam_init, ts):
    n, width = q.shape
    nb, n_pages = page_table.shape
    pages = _tile(n_pages, 8)
    n_groups = n_pages // pages
    page = cache_k.shape[2]
    hw = 2 * qk_dim
    kern = functools.partial(_paged_kernel, pages=pages, heads=heads, qk_dim=qk_dim, lam_init=lam_init,
                             n_groups=n_groups)
    tok = pl.BlockSpec((ts, width), lambda b, g, pt: (b, 0))

    def page_spec(p):
        return pl.BlockSpec((1, 1, page, width), lambda b, g, pt: (layer, pt[b, g * pages + p], 0, 0))

    grid_spec = pltpu.PrefetchScalarGridSpec(
        num_scalar_prefetch=1,
        grid=(nb, n_groups),
        in_specs=[pl.BlockSpec(lam.shape, lambda b, g, pt: (0, 0)), tok, tok, tok]
        + [page_spec(p) for p in range(pages)] * 2,
        out_specs=tok,
        scratch_shapes=[pltpu.VMEM((heads, 2 * ts, 1), F32), pltpu.VMEM((heads, 2 * ts, 1), F32),
                        pltpu.VMEM((heads, 2 * ts, hw), F32)],
    )
    return pl.pallas_call(
        kern,
        grid_spec=grid_spec,
        out_shape=jax.ShapeDtypeStruct((n, width), F32),
        compiler_params=_params("parallel", "arbitrary"),
        name="paged_attention",
    )(page_table, lam, q, k_new, v_new, *([cache_k] * pages), *([cache_v] * pages))


def _post_kernel(y_ref, r_ref, k_ref, v_ref, g_ref, o_ref, ga_ref, gb_ref, x_ref, g1_ref, sh2_ref, sc2_ref,
                 lng_ref, lnb_ref, rk_ref, sub_ref, n2_ref, ones_ref, wor_ref, wod_ref, wout_ref, *rest,
                 hs, v_dim, lam_init, n_experts, precise):
    if n_experts:
        router_ref, x_out_ref, h2_ref, gate_ref = rest
    else:
        x_out_ref, h2_ref = rest
    ones = ones_ref[...]
    y = y_ref[...]
    inv_hs = 1.0 / hs
    mu = _dot_exact_rhs(y, ones) * inv_hs
    yc = y - mu
    var = _dot_exact_rhs(yc * yc, ones) * inv_hs
    yn = yc * lax.rsqrt(var + LN_X_EPS) * lng_ref[...] + lnb_ref[...]
    v = v_ref[...]
    bonus = _dot_exact_rhs(r_ref[...] * k_ref[...] * rk_ref[...], ones) * v
    out_a = _mm((yn + bonus) * g_ref[...], wor_ref[...], precise)

    o = o_ref[...]
    parts = []
    for h in range(o.shape[1] // v_dim):
        oh = o[:, h * v_dim:(h + 1) * v_dim]
        parts.append(oh * lax.rsqrt(jnp.mean(oh * oh, axis=-1, keepdims=True) + NORM_EPS))
    on = jnp.concatenate(parts, axis=1) * sub_ref[...] * (1.0 - lam_init)
    out_b = _mm(on, wod_ref[...], precise)

    merged = _sigmoid(ga_ref[...]) * out_a + _sigmoid(gb_ref[...]) * out_b
    x = x_ref[...] + g1_ref[0] * _mm(merged, wout_ref[...], precise)
    x_out_ref[...] = x
    xn = x * lax.rsqrt(jnp.mean(x * x, axis=-1, keepdims=True) + NORM_EPS)
    h2 = (xn * n2_ref[...]) * (1.0 + sc2_ref[0]) + sh2_ref[0]
    h2_ref[...] = h2.astype(h2_ref.dtype)
    if n_experts:
        logits = _dot3(h2, router_ref[...])
        lane = _iota(logits.shape, 1).astype(F32)
        big = float(LANES)
        lg = jnp.where(lane < n_experts, logits, -jnp.inf)
        m1 = jnp.max(lg, axis=-1, keepdims=True)
        i1 = jnp.min(jnp.where(lg == m1, lane, big), axis=-1, keepdims=True)
        lg2 = jnp.where(lane == i1, -jnp.inf, lg)
        m2 = jnp.max(lg2, axis=-1, keepdims=True)
        i2 = jnp.min(jnp.where(lg2 == m2, lane, big), axis=-1, keepdims=True)
        e2 = jnp.exp(m2 - m1)
        den = 1.0 + e2
        gate_ref[...] = jnp.where(lane == i1, 1.0 / den, 0.0) + jnp.where(lane == i2, e2 / den, 0.0)


def _post_call(y, r, k, v, g, o, proj, x, rows, lw, ones_head, gate_cols, lam_init, hs, v_dim):
    n, d = x.shape
    width = y.shape[1]
    tm = _tile(n, 256)
    router = lw.get("router_pad")
    n_experts = lw["n_experts"] if router is not None else 0
    tok = lambda wd: pl.BlockSpec((tm, wd), lambda i: (i, 0))
    row_spec = lambda wd: pl.BlockSpec((1, wd), lambda i: (0, 0))
    full = lambda a: pl.BlockSpec(a.shape, lambda i: (0,) * a.ndim)
    in_specs = [tok(width)] * 6 + [
        pl.BlockSpec((tm, d), lambda i: (i, gate_cols[0])), pl.BlockSpec((tm, d), lambda i: (i, gate_cols[1])),
        tok(d), rows.mod_spec(tm, d, 2, 1), rows.mod_spec(tm, d, 3, 1), rows.mod_spec(tm, d, 4, 1),
        row_spec(width), row_spec(width), row_spec(width), row_spec(width), row_spec(d), full(ones_head),
        full(lw["w_o_rwkv"]), full(lw["w_o_diff"]), full(lw["w_out"]),
    ]
    args = [y, r, k, v, g, o, proj, proj, x, rows.mod3, rows.mod3, rows.mod3,
            lw["ln_x_g"], lw["ln_x_b"], lw["r_k"], lw["subln_g"], lw["norm2_g"], ones_head,
            lw["w_o_rwkv"], lw["w_o_diff"], lw["w_out"]]
    out_specs = [tok(d), tok(d)]
    precise = lw["w_out"].dtype == F32
    out_shape = [jax.ShapeDtypeStruct((n, d), F32), jax.ShapeDtypeStruct((n, d), F32 if precise else BF16)]
    if n_experts:
        in_specs.append(full(router))
        args.append(router)
        out_specs.append(tok(LANES))
        out_shape.append(jax.ShapeDtypeStruct((n, LANES), F32))
    kern = functools.partial(_post_kernel, hs=hs, v_dim=v_dim, lam_init=lam_init, n_experts=n_experts,
                             precise=precise)
    return pl.pallas_call(
        kern,
        grid=(n // tm,),
        in_specs=in_specs,
        out_specs=out_specs,
        out_shape=out_shape,
        compiler_params=_params("parallel"),
        name="merge_post",
    )(*args)


def _ffn_kernel(*refs, moe, precise):
    if moe:
        h_ref, x_ref, g2_ref, gate_ref, w1_ref, w3_ref, w2_ref, o_ref, acc_scr = refs
    else:
        h_ref, x_ref, g2_ref, w1_ref, w3_ref, w2_ref, o_ref, acc_scr = refs
    e = pl.program_id(1)
    f = pl.program_id(2)

    @pl.when((e == 0) & (f == 0))
    def _():
        acc_scr[...] = jnp.zeros(acc_scr.shape, F32)

    h = h_ref[...]
    a = _mm(h, w1_ref[0], precise)
    b = _mm(h, w3_ref[0], precise)
    act = a * _sigmoid(a) * b
    if moe:
        gate = gate_ref[...]
        lane = _iota(gate.shape, 1)
        act = act * jnp.sum(jnp.where(lane == e, gate, 0.0), axis=-1, keepdims=True)
    acc_scr[...] += _mm(act, w2_ref[0], precise)

    @pl.when((e == pl.num_programs(1) - 1) & (f == pl.num_programs(2) - 1))
    def _():
        o_ref[...] = x_ref[...] + g2_ref[0] * acc_scr[...]


def _ffn_call(h2, x, rows, gate, w1, w3, w2, tf):
    n, d = x.shape
    n_exp, _, ff = w1.shape
    tm = _tile(n, 512 if w1.dtype == BF16 else 256)
    moe = gate is not None
    tok = lambda wd: pl.BlockSpec((tm, wd), lambda i, e, f: (i, 0))
    in_specs = [tok(d), tok(d), rows.mod_spec(tm, d, 5, 3)]
    args = [h2, x, rows.mod3]
    if moe:
        in_specs.append(tok(LANES))
        args.append(gate)
    in_specs += [pl.BlockSpec((1, d, tf), lambda i, e, f: (e, 0, f)),
                 pl.BlockSpec((1, d, tf), lambda i, e, f: (e, 0, f)),
                 pl.BlockSpec((1, tf, d), lambda i, e, f: (e, f, 0))]
    args += [w1, w3, w2]
    return pl.pallas_call(
        functools.partial(_ffn_kernel, moe=moe, precise=w1.dtype == F32),
        grid=(n // tm, n_exp, ff // tf),
        in_specs=in_specs,
        out_specs=tok(d),
        out_shape=jax.ShapeDtypeStruct((n, d), F32),
        scratch_shapes=[pltpu.VMEM((tm, d), F32)],
        compiler_params=_params("parallel", "arbitrary", "arbitrary"),
        name="moe_ffn" if moe else "dense_ffn",
    )(*args)


def _ff_tile(ff):
    for t in (1408, 896, 512, 256, 128):
        if ff % t == 0 and ff // t >= 2:
            return t
    return ff


def _block_ones(width, seg):
    i = jnp.arange(width) // seg
    return (i[:, None] == i[None, :]).astype(BF16)


def _rope_tables(pos, qk_dim, n_seg):
    half = qk_dim // 2
    inv_freq = ROPE_THETA ** (-jnp.arange(half, dtype=F32) / half)
    ang = pos.astype(F32)[:, None] * inv_freq[None, :]
    cos, sin = jnp.cos(ang), jnp.sin(ang)
    cos_t = jnp.tile(jnp.concatenate([cos, cos], axis=1), (1, n_seg))
    sin_t = jnp.tile(jnp.concatenate([-sin, sin], axis=1), (1, n_seg))
    return cos_t, sin_t


def kernel(x_prompt, x_sample, c_prompt, c_sample, cache_k, cache_v, state_wkv, state_shift, page_table,
           norm1_g, norm2_g, w_ada, b_ada, w_in, mu_shift, w0, w_up, a0, a_up, g_up, k_k, k_a, r_k,
           ln_x_g, ln_x_b, w_o_rwkv, q_norm_g, k_norm_g, lam, subln_g, w_o_diff, w_out,
           ffn_w1, ffn_w3, ffn_w2, router, moe_w1, moe_w3, moe_w2):
    n_batch, seq, d = x_prompt.shape
    n_dec, dec_seq, _ = x_sample.shape
    depth = w_in.shape[0]
    heads_r, hs = r_k.shape[1], r_k.shape[2]
    width_r = heads_r * hs
    proj_r = mu_shift.shape[-1]
    n_phys, page, heads_d, _, qk_dim = cache_k.shape[1:]
    v_dim = cache_v.shape[-1]
    qk_w = heads_d * 2 * qk_dim
    v_w = heads_d * v_dim
    w_lora, a_lora = w_up.shape[1], a_up.shape[1]
    assert v_dim == 2 * qk_dim == LANES and qk_w == v_w and w_lora + a_lora == LANES
    past_len = page_table.shape[1] * page
    n_p, n_s = n_batch * seq, n_dec * dec_seq

    o_q = proj_r
    o_k, o_v, o_g = o_q + qk_w, o_q + 2 * qk_w, o_q + 2 * qk_w + v_w
    w_in_f = jnp.concatenate([w_in[:, :, o_g:], w_in[:, :, o_q:o_g], w_in[:, :, :o_q]], axis=-1)
    w_in_b = w_in_f.astype(BF16)
    n_gate = w_in.shape[-1] - o_g
    assert n_gate == 2 * d and (n_gate + 3 * qk_w) % proj_r == 0 and n_gate % qk_w == 0
    gate_cols = (0, 1)
    qkv_cols = tuple(n_gate // qk_w + j for j in range(3))
    rwkv_col = (n_gate + 3 * qk_w) // proj_r
    tn = proj_r

    c_all = jnp.concatenate([c_prompt, c_sample], axis=0)
    c_all = jnp.pad(c_all, ((0, (-c_all.shape[0]) % 8), (0, 0)))
    mod = _ada_call(c_all, w_ada, b_ada)

    ones_head = _block_ones(width_r, hs)
    ones_qk = _block_ones(qk_w, qk_dim)
    cos_p, sin_p = _rope_tables(jnp.arange(seq), qk_dim, qk_w // qk_dim)
    cos_s, sin_s = _rope_tables(past_len + jnp.arange(dec_seq), qk_dim, qk_w // qk_dim)
    cos_s, sin_s = jnp.tile(cos_s, (n_dec, 1)), jnp.tile(sin_s, (n_dec, 1))
    cache_k4 = cache_k.reshape(depth, n_phys, page, qk_w)
    cache_v4 = cache_v.reshape(depth, n_phys, page, v_w)

    zpad = lambda m, before, after: jnp.pad(m, ((before, after), (0, 0)))
    bf = lambda a: a.astype(BF16)
    xp = x_prompt.reshape(n_p, d)
    xs = x_sample.reshape(n_s, d)
    shift0 = jnp.zeros((n_batch, 1, proj_r), F32)
    wkv_zero = jnp.zeros((n_batch, heads_r, hs, hs), F32)
    wkv_chunk = _tile(seq, 64)
    dec_chunk = max(16, dec_seq)
    outs = {key: [] for key in ("kp", "vp", "ks", "vs", "wp", "ws", "sp", "ss")}

    for l in range(depth):
        lam_init = 0.8 - 0.6 * math.exp(-0.3 * l)
        lw = {
            "mu_shift": mu_shift[l][None], "w0": w0[l][None], "a0": a0[l][None],
            "w_up_pad": zpad(w_up[l], 0, a_lora), "a_up_pad": zpad(a_up[l], w_lora, 0), "g_up": g_up[l],
            "k_k": k_k[l][None], "k_a": k_a[l][None], "r_k": r_k[l].reshape(1, width_r),
            "ln_x_g": ln_x_g[l][None], "ln_x_b": ln_x_b[l][None],
            "subln_g": jnp.tile(subln_g[l], heads_d)[None], "norm2_g": norm2_g[l][None],
        }
        qg = jnp.tile(q_norm_g[l].reshape(-1), heads_d)[None]
        kg = jnp.tile(k_norm_g[l].reshape(-1), heads_d)[None]
        if l % 2 == 0:
            i = l // 2
            ffn_f = (ffn_w1[i][None], ffn_w3[i][None], ffn_w2[i][None])
        else:
            i = l // 2
            n_experts = router.shape[-1]
            lw["router_pad"] = jnp.pad(router[i], ((0, 0), (0, LANES - n_experts)))
            lw["n_experts"] = n_experts
            ffn_f = (moe_w1[i], moe_w3[i], moe_w2[i])
        tf = _ff_tile(ffn_f[0].shape[-1])
        norm1 = norm1_g[l][None]
        out_w = {"w_o_rwkv": w_o_rwkv[l], "w_o_diff": w_o_diff[l], "w_out": w_out[l]}
        lw_p = dict(lw, **{key: bf(val) for key, val in out_w.items()})
        lw_s = dict(lw, **out_w)
        ffn_b = tuple(bf(w) for w in ffn_f)

        def run_group(x, rows, init_rows, state0, is_prompt):
            lw_g, w_in_g, ffn = (lw_p, w_in_b[l], ffn_b) if is_prompt else (lw_s, w_in_f[l], ffn_f)
            proj = _proj_call(x, rows, norm1, w_in_g, tn)
            r, ld, k2, v_r, av, bv, g = _rwkv_prep_call(proj, rows, init_rows, lw_g, rwkv_col, ones_head,
                                                        not is_prompt)
            if is_prompt:
                y, state = _wkv_call(r, ld, k2, v_r, av, bv, state0, rows.seq_len, wkv_chunk)
                k_rows, v_rows, q_bf, k_bf, v_bf = _attn_prep_call(
                    proj, rows, cos_p, sin_p, qg, kg, ones_qk, qkv_cols, BF16, qk_dim)
                o = _flash_call(q_bf, k_bf, v_bf, lam[l], n_batch, seq, heads_d, qk_dim, lam_init)
            else:
                padt = lambda a: jnp.pad(a.reshape(n_dec, dec_seq, width_r),
                                         ((0, 0), (0, dec_chunk - dec_seq), (0, 0))).reshape(-1, width_r)
                y, state = _wkv_call(padt(r), padt(ld), padt(k2), padt(v_r), padt(av), padt(bv), state0,
                                     dec_chunk, dec_chunk)
                y = y.reshape(n_dec, dec_chunk, width_r)[:, :dec_seq].reshape(-1, width_r)
                k_rows, v_rows, q_f, _, _ = _attn_prep_call(
                    proj, rows, cos_s, sin_s, qg, kg, ones_qk, qkv_cols, F32, qk_dim)
                o = _paged_call(page_table, lam[l], q_f, k_rows, v_rows, cache_k4, cache_v4, l, heads_d,
                                qk_dim, lam_init, dec_seq)
            post = _post_call(y, r, k2, v_r, g, o, proj, x, rows, lw_g, ones_head, gate_cols, lam_init, hs, v_dim)
            x_new, h2 = post[0], post[1]
            gate = post[2] if len(post) > 2 else None
            x_out = _ffn_call(h2, x_new, rows, gate, *ffn, tf)
            shift = proj.reshape(-1, rows.seq_len, proj.shape[-1])[:, -1, rwkv_col * proj_r:]
            return x_out, k_rows, v_rows, state, shift

        rows_p = _Rows(n_p, seq, mod[l, :n_batch][:, None, :])
        rows_s = _Rows(n_s, dec_seq, jnp.repeat(mod[l, n_batch:n_batch + n_dec], dec_seq, axis=0)[None])
        xp, kr, vr, st, sh = run_group(xp, rows_p, shift0, wkv_zero, True)
        outs["kp"].append(kr.reshape(n_batch, seq, heads_d, 2, qk_dim))
        outs["vp"].append(vr.reshape(n_batch, seq, heads_d, v_dim))
        outs["wp"].append(st)
        outs["sp"].append(sh)
        init_s = jnp.repeat(state_shift[l], dec_seq, axis=0)
        xs, kr, vr, st, sh = run_group(xs, rows_s, init_s, state_wkv[l], False)
        outs["ks"].append(kr.reshape(n_dec, dec_seq, heads_d, 2, qk_dim))
        outs["vs"].append(vr.reshape(n_dec, dec_seq, heads_d, v_dim))
        outs["ws"].append(st)
        outs["ss"].append(sh)

    st = lambda key: jnp.stack(outs[key])
    return (xp.reshape(n_batch, seq, d), xs.reshape(n_dec, dec_seq, d), st("kp"), st("vp"), st("ks"), st("vs"),
            st("wp"), st("ws"), st("sp"), st("ss"))
```

```python
import functools
import math

import jax
import jax.numpy as jnp
from jax import lax
from jax.experimental import pallas as pl
from jax.experimental.pallas import tpu as pltpu

F32 = jnp.float32
BF16 = jnp.bfloat16

NORM_EPS = 1e-6
LN_X_EPS = 64e-5
ROPE_THETA = 10000.0
TOP_K = 2
KK_NORM_FLOOR = 1e-12
LANES = 128
VMEM_LIMIT = 56 * 1024 * 1024


def _params(*sem):
    return pltpu.CompilerParams(dimension_semantics=sem, vmem_limit_bytes=VMEM_LIMIT)


def _tile(n, pref):
    if n <= pref:
        return n
    t = pref
    while n % t:
        t //= 2
    return t


def _bdot(a, b):
    return jnp.dot(a.astype(BF16), b.astype(BF16), preferred_element_type=F32)


def _mm(a, b, precise):
    return _dot3(a.astype(F32), b.astype(F32)) if precise else _bdot(a, b)


def _dot_nt(a, b):
    return lax.dot_general(a, b, (((1,), (1,)), ((), ())), preferred_element_type=F32)


def _split2(x):
    hi = x.astype(BF16)
    lo = (x - hi.astype(F32)).astype(BF16)
    return hi, lo


def _split3(x):
    hi = x.astype(BF16)
    r1 = x - hi.astype(F32)
    mid = r1.astype(BF16)
    lo = (r1 - mid.astype(F32)).astype(BF16)
    return hi, mid, lo


def _dot3(a, b):
    ah, al = _split2(a)
    bh, bl = _split2(b)
    d = functools.partial(jnp.dot, preferred_element_type=F32)
    return d(ah, bh) + d(ah, bl) + d(al, bh)


def _dot3_nt(a, b):
    ah, al = _split2(a)
    bh, bl = _split2(b)
    return _dot_nt(ah, bh) + _dot_nt(ah, bl) + _dot_nt(al, bh)


def _dot_exact_rhs(x, m_bf):
    hi, mid, lo = _split3(x)
    d = functools.partial(jnp.dot, preferred_element_type=F32)
    return d(hi, m_bf) + d(mid, m_bf) + d(lo, m_bf)


def _dot_exact_lhs(m_bf, x):
    hi, mid, lo = _split3(x)
    d = functools.partial(jnp.dot, preferred_element_type=F32)
    return d(m_bf, hi) + d(m_bf, mid) + d(m_bf, lo)


def _sigmoid(x):
    return 1.0 / (1.0 + jnp.exp(-x))


def _softplus(x):
    return jnp.maximum(x, 0.0) + jnp.log(1.0 + jnp.exp(-jnp.abs(x)))


def _iota(shape, dim):
    return lax.broadcasted_iota(jnp.int32, shape, dim)


def _ada_kernel(c_ref, w_ref, b_ref, o_ref):
    o_ref[0] = _dot3(c_ref[...], w_ref[0]) + b_ref[0]


def _ada_call(c_all, w_ada, b_ada):
    n_layers, d, d6 = w_ada.shape
    m = c_all.shape[0]
    tn = _tile(d6, 1536)
    return pl.pallas_call(
        _ada_kernel,
        grid=(n_layers, d6 // tn),
        in_specs=[
            pl.BlockSpec((m, d), lambda l, j: (0, 0)),
            pl.BlockSpec((1, d, tn), lambda l, j: (l, 0, j)),
            pl.BlockSpec((1, 1, tn), lambda l, j: (l, 0, j)),
        ],
        out_specs=pl.BlockSpec((1, m, tn), lambda l, j: (l, 0, j)),
        out_shape=jax.ShapeDtypeStruct((n_layers, m, d6), F32),
        compiler_params=_params("parallel", "parallel"),
        name="adaln_mod",
    )(c_all, w_ada, b_ada.reshape(n_layers, 1, d6))


class _Rows:
    def __init__(self, n, seq_len, mod3):
        self.n, self.seq_len, self.mod3 = n, seq_len, mod3
        self.per_row = mod3.shape[1] != 1

    def mod_spec(self, tm, d, col, n_grid):
        r = self.mod3.shape[1]
        if self.per_row:
            assert r == tm == self.n
            idx = lambda i, *_: (0, 0, col)
        else:
            assert self.seq_len % tm == 0
            per = self.seq_len // tm
            idx = lambda i, *_: (i // per, 0, col)
        return pl.BlockSpec((1, r, d), idx)


def _proj_kernel(x_ref, sh_ref, sc_ref, g_ref, w_ref, o_ref, h_scr, *, precise):
    @pl.when(pl.program_id(1) == 0)
    def _():
        x = x_ref[...]
        y = x * lax.rsqrt(jnp.mean(x * x, axis=-1, keepdims=True) + NORM_EPS)
        h = (y * g_ref[...]) * (1.0 + sc_ref[0]) + sh_ref[0]
        h_scr[...] = h.astype(h_scr.dtype)

    o_ref[...] = _mm(h_scr[...], w_ref[...], precise)


def _proj_call(x, rows, norm_g, w, tn):
    n, d = x.shape
    pw = w.shape[1]
    tm = _tile(n, 512)
    precise = w.dtype == F32
    return pl.pallas_call(
        functools.partial(_proj_kernel, precise=precise),
        grid=(n // tm, pw // tn),
        in_specs=[
            pl.BlockSpec((tm, d), lambda i, j: (i, 0)),
            rows.mod_spec(tm, d, 0, 2),
            rows.mod_spec(tm, d, 1, 2),
            pl.BlockSpec((1, d), lambda i, j: (0, 0)),
            pl.BlockSpec((d, tn), lambda i, j: (0, j)),
        ],
        out_specs=pl.BlockSpec((tm, tn), lambda i, j: (i, j)),
        out_shape=jax.ShapeDtypeStruct((n, pw), F32),
        scratch_shapes=[pltpu.VMEM((tm, d), w.dtype)],
        compiler_params=_params("parallel", "arbitrary"),
        name="proj_in",
    )(x, rows.mod3, rows.mod3, norm_g, w)


def _rwkv_prep_kernel(cols_ref, prev8_ref, init_ref, mu_ref, w0_ref, wup_ref, a0_ref, aup_ref,
                      gup_ref, kk_ref, ka_ref, ones_ref,
                      r_ref, ld_ref, k_ref, v_ref, av_ref, bv_ref, g_ref, *, seq_len, tm, width, precise):
    cols = cols_ref[...]
    rolled = pltpu.roll(cols, 1, 0)
    row = _iota((tm, 1), 0)
    if seq_len >= tm:
        first_tile = (pl.program_id(0) % (seq_len // tm)) == 0
        boundary = jnp.where(first_tile, init_ref[0], prev8_ref[7:8, :])
        prev = jnp.where(row == 0, boundary, rolled)
    else:
        prev = jnp.where(row % seq_len == 0, init_ref[...], rolled)
    xs = cols + (prev - cols) * mu_ref[...]
    w = width
    r = xs[:, 0:w]
    k = xs[:, w:2 * w]
    v = xs[:, 2 * w:3 * w]
    lora_wa = xs[:, 3 * w:3 * w + LANES]
    gd = xs[:, 3 * w + LANES:]
    w_log = -_softplus(-(w0_ref[...] + _mm(jnp.tanh(lora_wa), wup_ref[...], precise))) - 0.5
    a = _sigmoid(a0_ref[...] + _mm(lora_wa, aup_ref[...], precise))
    g = _mm(_sigmoid(gd), gup_ref[...], precise)
    kk = k * kk_ref[...]
    ss = _dot_exact_rhs(kk * kk, ones_ref[...])
    kk = kk / jnp.maximum(jnp.sqrt(ss), KK_NORM_FLOOR)
    r_ref[...] = r
    ld_ref[...] = -jnp.exp(w_log)
    k_ref[...] = k * (1.0 + (a - 1.0) * ka_ref[...])
    v_ref[...] = v
    av_ref[...] = -kk
    bv_ref[...] = kk * a
    g_ref[...] = g


def _rwkv_prep_call(proj, rows, init_rows, lw, col_block, ones_head, precise):
    n = proj.shape[0]
    width = lw["w0"].shape[-1]
    pw = lw["mu_shift"].shape[-1]
    tm = _tile(n, 256) if rows.seq_len >= 256 else n
    assert pw == 3 * width + LANES + lw["g_up"].shape[0]
    if rows.seq_len >= tm:
        init_spec = pl.BlockSpec((1, 1, pw), lambda i: ((i * tm) // rows.seq_len, 0, 0))
    else:
        init_spec = pl.BlockSpec((tm, pw), lambda i: (0, 0))
    row_spec = lambda wd: pl.BlockSpec((1, wd), lambda i: (0, 0))
    full = lambda a: pl.BlockSpec(a.shape, lambda i: (0,) * a.ndim)
    out = jax.ShapeDtypeStruct((n, width), F32)
    kern = functools.partial(_rwkv_prep_kernel, seq_len=rows.seq_len, tm=tm, width=width, precise=precise)
    return pl.pallas_call(
        kern,
        grid=(n // tm,),
        in_specs=[
            pl.BlockSpec((tm, pw), lambda i: (i, col_block)),
            pl.BlockSpec((8, pw), lambda i: (jnp.maximum(i * (tm // 8) - 1, 0), col_block)),
            init_spec,
            row_spec(pw), row_spec(width), full(lw["w_up_pad"]), row_spec(width), full(lw["a_up_pad"]),
            full(lw["g_up"]), row_spec(width), row_spec(width), full(ones_head),
        ],
        out_specs=[pl.BlockSpec((tm, width), lambda i: (i, 0))] * 7,
        out_shape=[out] * 7,
        compiler_params=_params("parallel"),
        name="rwkv_prep",
    )(proj, proj, init_rows, lw["mu_shift"], lw["w0"], lw["w_up_pad"], lw["a0"], lw["a_up_pad"],
      lw["g_up"], lw["k_k"], lw["k_a"], ones_head)


def _wkv_kernel(r_ref, ld_ref, k_ref, v_ref, a_ref, b_ref, s0_ref, y_ref, s_ref, *, chunk, heads, hs, precise):
    c = chunk
    if precise:
        mm, mm_nt = _dot3, _dot3_nt
    else:
        mm = _bdot
        mm_nt = lambda a, b: _dot_nt(a.astype(BF16), b.astype(BF16))

    @pl.when(pl.program_id(1) == 0)
    def _():
        s_ref[...] = s0_ref[...]

    rr = _iota((c, c), 0)
    cc = _iota((c, c), 1)
    strict = rr > cc
    incl = rr >= cc
    tri = incl.astype(BF16)
    eye_c = (rr == cc).astype(F32)
    eye_v = (_iota((hs, hs), 0) == _iota((hs, hs), 1)).astype(BF16)

    ld = ld_ref[...]
    cum = _dot_exact_lhs(tri, ld)
    tot = cum[c - 1:c, :]
    e_neg = jnp.exp(-cum)
    e_dec = jnp.exp(tot - cum)
    a_t = a_ref[...] * jnp.exp(cum - ld)
    r_t = r_ref[...] * jnp.exp(cum)
    b_t = b_ref[...] * e_neg
    k_t = k_ref[...] * e_neg
    b_d = b_ref[...] * e_dec
    k_d = k_ref[...] * e_dec
    e_tot = jnp.exp(tot)
    v_all = v_ref[...]

    n_double = int(math.log2(c)) - 1
    hr = range(heads)
    sls = [slice(h * hs, (h + 1) * hs) for h in hr]
    pm = [mm_nt(jnp.concatenate([a_t[:, sl], r_t[:, sl]], axis=0),
                jnp.concatenate([b_t[:, sl], k_t[:, sl]], axis=0)) for sl in sls]
    l_ab = [jnp.where(strict, p[:c, :c], 0.0) for p in pm]
    l_ak = [jnp.where(strict, p[:c, c:], 0.0) for p in pm]
    m_rb = [jnp.where(incl, p[c:, :c], 0.0) for p in pm]
    m_rk = [jnp.where(incl, p[c:, c:], 0.0) for p in pm]
    t_inv = [eye_c + l for l in l_ab]
    l_pow = l_ab
    for _ in range(n_double):
        l_pow = [mm(l, l) for l in l_pow]
        t_inv = [t + mm(t, l) for t, l in zip(t_inv, l_pow)]
    vh = [v_all[:, sl] for sl in sls]
    z = [mm(l_ak[h], vh[h]) for h in hr]
    au = [mm(t_inv[h], jnp.concatenate([a_t[:, sls[h]], z[h]], axis=1)) for h in hr]
    s0 = [s_ref[0, h] for h in hr]
    u = [mm_nt(au[h][:, :hs], s0[h]) + au[h][:, hs:] for h in hr]
    y = [mm_nt(r_t[:, sls[h]], s0[h]) + mm(m_rb[h], u[h]) + mm(m_rk[h], vh[h]) for h in hr]
    for h in hr:
        uv = jnp.concatenate([u[h], vh[h]], axis=0)
        if precise:
            uh, um, ul = _split3(uv)
            uv_t = _dot_nt(eye_v, uh) + _dot_nt(eye_v, um) + _dot_nt(eye_v, ul)
        else:
            uv_t = _dot_nt(eye_v, uv.astype(BF16))
        bkd = jnp.concatenate([b_d[:, sls[h]], k_d[:, sls[h]]], axis=0)
        s_ref[0, h] = s0[h] * e_tot[:, sls[h]] + mm(uv_t, bkd)
        y_ref[:, sls[h]] = y[h]


def _wkv_call(r, ld, k, v, av, bv, state0, seq_len, chunk, precise):
    n, width = r.shape
    nb, heads, hs, _ = state0.shape
    n_chunks = seq_len // chunk
    tok = pl.BlockSpec((chunk, width), lambda b, c: (b * n_chunks + c, 0))
    st = pl.BlockSpec((1, heads, hs, hs), lambda b, c: (b, 0, 0, 0))
    kern = functools.partial(_wkv_kernel, chunk=chunk, heads=heads, hs=hs, precise=precise)
    return pl.pallas_call(
        kern,
        grid=(nb, n_chunks),
        in_specs=[tok] * 6 + [st],
        out_specs=[tok, st],
        out_shape=[jax.ShapeDtypeStruct((n, width), F32), jax.ShapeDtypeStruct(state0.shape, F32)],
        compiler_params=_params("parallel", "arbitrary"),
        name="wkv_scan",
    )(r, ld, k, v, av, bv, state0)


def _attn_prep_kernel(q_ref, k_ref, v_ref, cos_ref, sin_ref, qg_ref, kg_ref, ones_ref,
                      kr_ref, vr_ref, qo_ref, kb_ref, vb_ref, *, qk_dim, scale):
    ones = ones_ref[...]
    cos = cos_ref[...]
    sin = sin_ref[...]
    lane = _iota(cos.shape, 1)
    first_half = (lane % qk_dim) < (qk_dim // 2)
    width = cos.shape[1]
    half = qk_dim // 2

    def norm_rope(x, g):
        ms = _dot_exact_rhs(x * x, ones) * (1.0 / qk_dim)
        y = x * lax.rsqrt(ms + NORM_EPS) * g
        swapped = jnp.where(first_half, pltpu.roll(y, width - half, 1), pltpu.roll(y, half, 1))
        return y * cos + swapped * sin

    q = norm_rope(q_ref[...], qg_ref[...])
    k = norm_rope(k_ref[...], kg_ref[...])
    v = v_ref[...]
    kr_ref[...] = k
    vr_ref[...] = v
    qo_ref[...] = (q * scale).astype(qo_ref.dtype)
    kb_ref[...] = k.astype(BF16)
    vb = v.astype(BF16)
    hw = 2 * qk_dim
    one = jnp.ones((vb.shape[0], hw), BF16)
    vb_ref[...] = jnp.concatenate(
        [blk for h in range(width // hw) for blk in (vb[:, h * hw:(h + 1) * hw], one)], axis=1)


def _attn_prep_call(proj, rows, cos, sin, qg, kg, ones_qk, col_blocks, q_dtype, qk_dim):
    n = proj.shape[0]
    width = qg.shape[-1]
    tm = _tile(n, 512) if rows.seq_len >= 512 else n
    per = max(rows.seq_len // tm, 1)
    tab = pl.BlockSpec((tm, width), lambda i: (i % per, 0))
    col = lambda cb: pl.BlockSpec((tm, width), lambda i: (i, cb))
    row_spec = pl.BlockSpec((1, width), lambda i: (0, 0))
    out_spec = pl.BlockSpec((tm, width), lambda i: (i, 0))
    kern = functools.partial(_attn_prep_kernel, qk_dim=qk_dim, scale=qk_dim ** -0.5)
    sds = lambda dt: jax.ShapeDtypeStruct((n, width), dt)
    return pl.pallas_call(
        kern,
        grid=(n // tm,),
        in_specs=[col(col_blocks[0]), col(col_blocks[1]), col(col_blocks[2]), tab, tab, row_spec, row_spec,
                  pl.BlockSpec(ones_qk.shape, lambda i: (0, 0))],
        out_specs=[out_spec] * 4 + [pl.BlockSpec((tm, 2 * width), lambda i: (i, 0))],
        out_shape=[sds(F32), sds(F32), sds(q_dtype), sds(BF16), jax.ShapeDtypeStruct((n, 2 * width), BF16)],
        compiler_params=_params("parallel"),
        name="attn_prep",
    )(proj, proj, proj, cos, sin, qg, kg, ones_qk)


def _lambda(lam_ref, lam_init):
    lp = lam_ref[...]
    s1 = jnp.sum(lp[0:1] * lp[1:2], axis=-1, keepdims=True)
    s2 = jnp.sum(lp[2:3] * lp[3:4], axis=-1, keepdims=True)
    return jnp.exp(s1) - jnp.exp(s2) + lam_init


def _stack_maps(qh, qk_dim):
    lane = _iota(qh.shape, 1)
    zero = jnp.zeros_like(qh)
    return jnp.concatenate([jnp.where(lane < qk_dim, qh, zero), jnp.where(lane >= qk_dim, qh, zero)], axis=0)


def _flash_kernel(lam_ref, q_ref, k_ref, v_ref, o_ref, qs_scr, m_scr, acc_scr, *, tq, tk, qk_dim, lam_init):
    i = pl.program_id(2)
    rows = 2 * tq
    hw = 2 * qk_dim
    n_blk = tk // hw
    qs_scr[...] = _stack_maps(q_ref[...], qk_dim)
    m_scr[...] = jnp.full(m_scr.shape, -jnp.inf, F32)
    acc_scr[...] = jnp.zeros(acc_scr.shape, F32)
    n_full = (i * tq) // tk

    def step(j, masked):
        start = pl.multiple_of(j * tk, tk)
        kc = k_ref[pl.ds(start, tk), :]
        vc = v_ref[pl.ds(start, tk), :]
        s = _dot_nt(qs_scr[...], kc)
        if masked:
            qpos = i * tq + _iota((tq, tk), 0)
            kpos = start + _iota((tq, tk), 1)
            keep = kpos <= qpos
            s = jnp.where(jnp.concatenate([keep, keep], axis=0), s, -jnp.inf)
        blocks = [s[:, c * hw:(c + 1) * hw] for c in range(n_blk)]
        blk_max = blocks[0]
        for blk in blocks[1:]:
            blk_max = jnp.maximum(blk_max, blk)
        m_prev = m_scr[...]
        m_new = jnp.maximum(m_prev, jnp.max(blk_max, axis=-1, keepdims=True))
        alpha = jnp.exp(m_prev - m_new)
        p = jnp.concatenate([jnp.exp(blk - m_new).astype(BF16) for blk in blocks], axis=1)
        pv = jnp.dot(p, vc, preferred_element_type=F32)
        acc_scr[...] = jnp.concatenate([alpha, alpha], axis=1) * acc_scr[...] + pv
        m_scr[...] = m_new

    def body(j, carry):
        step(j, False)
        return carry

    lax.fori_loop(0, n_full, body, 0)
    step(n_full, True)
    acc = acc_scr[...]
    o = acc[:, :hw] / acc[:, hw:]
    lam = _lambda(lam_ref, lam_init)
    o_ref[...] = o[:tq] - lam * o[tq:]


def _flash_call(q_bf, k_bf, v_aug, lam, n_batch, seq_len, heads, qk_dim, lam_init):
    n, width = q_bf.shape
    hw = width // heads
    tq = _tile(seq_len, 256)
    tk = _tile(seq_len, 1024)
    nq = seq_len // tq
    kern = functools.partial(_flash_kernel, tq=tq, tk=tk, qk_dim=qk_dim, lam_init=lam_init)
    qspec = pl.BlockSpec((tq, hw), lambda b, h, i: (b * nq + i, h))
    return pl.pallas_call(
        kern,
        grid=(n_batch, heads, nq),
        in_specs=[pl.BlockSpec(lam.shape, lambda b, h, i: (0, 0)), qspec,
                  pl.BlockSpec((seq_len, hw), lambda b, h, i: (b, h)),
                  pl.BlockSpec((seq_len, 2 * hw), lambda b, h, i: (b, h))],
        out_specs=qspec,
        out_shape=jax.ShapeDtypeStruct((n, width), F32),
        scratch_shapes=[pltpu.VMEM((2 * tq, hw), BF16), pltpu.VMEM((2 * tq, hw), F32),
                        pltpu.VMEM((2 * tq, 2 * hw), F32)],
        compiler_params=_params("parallel", "parallel", "arbitrary"),
        name="prompt_attention",
    )(lam, q_bf, k_bf, v_aug)


def _paged_kernel(pt_ref, lam_ref, q_ref, kn_ref, vn_ref, *refs, pages, heads, qk_dim, lam_init, n_groups):
    del pt_ref
    k_refs = refs[:pages]
    v_refs = refs[pages:2 * pages]
    o_ref = refs[2 * pages]
    m_scr, l_scr, acc_scr = refs[2 * pages + 1:]
    g = pl.program_id(1)
    hw = 2 * qk_dim
    ts = q_ref.shape[0]

    @pl.when(g == 0)
    def _():
        m_scr[...] = jnp.full(m_scr.shape, -jnp.inf, F32)
        l_scr[...] = jnp.zeros(l_scr.shape, F32)
        acc_scr[...] = jnp.zeros(acc_scr.shape, F32)

    q = q_ref[...]

    def update(h, s, pv_fn):
        m_prev = m_scr[h]
        m_new = jnp.maximum(m_prev, jnp.max(s, axis=-1, keepdims=True))
        alpha = jnp.exp(m_prev - m_new)
        p = jnp.exp(s - m_new)
        l_scr[h] = alpha * l_scr[h] + jnp.sum(p, axis=-1, keepdims=True)
        acc_scr[h] = alpha * acc_scr[h] + pv_fn(p)
        m_scr[h] = m_new

    def scores(qs, keys):
        qh, ql = _split2(qs)
        kh, kl = _split2(keys)
        both = _dot_nt(jnp.concatenate([qh, ql], axis=0), kh)
        return both[:2 * ts] + both[2 * ts:] + _dot_nt(qh, kl)

    def weighted(p, vals):
        ph, plo = _split2(p)
        vh, vl = _split2(vals)
        both = jnp.dot(jnp.concatenate([ph, plo], axis=0), vh, preferred_element_type=F32)
        return both[:2 * ts] + both[2 * ts:] + jnp.dot(ph, vl, preferred_element_type=F32)

    def scores_t(qs, keys_t):
        qh, ql = _split2(qs)
        kh, kl = _split2(keys_t)
        both = jnp.dot(jnp.concatenate([qh, ql], axis=0), kh, preferred_element_type=F32)
        return both[:2 * ts] + both[2 * ts:] + jnp.dot(qh, kl, preferred_element_type=F32)

    page = k_refs[0].shape[-1]
    for h in range(heads):
        sl = slice(h * hw, (h + 1) * hw)
        qs = _stack_maps(q[:, sl], qk_dim)
        s = jnp.concatenate([scores_t(qs, k_refs[p][0, 0, h].reshape(hw, page)) for p in range(pages)], axis=1)

        def pv(p_all, h=h):
            acc = None
            for p in range(pages):
                vals = v_refs[p][0, 0, pl.ds(h, page, stride=heads), :]
                t = weighted(p_all[:, p * page:(p + 1) * page], vals)
                acc = t if acc is None else acc + t
            return acc

        update(h, s, pv)

    @pl.when(g == n_groups - 1)
    def _():
        lam = _lambda(lam_ref, lam_init)
        pad = jnp.zeros((LANES - ts, kn_ref.shape[1]), F32)
        kn = jnp.concatenate([kn_ref[...], pad], axis=0)
        vn = jnp.concatenate([vn_ref[...], pad], axis=0)
        qpos = _iota((2 * ts, LANES), 0) % ts
        kpos = _iota((2 * ts, LANES), 1)
        keep = kpos <= qpos
        for h in range(heads):
            sl = slice(h * hw, (h + 1) * hw)
            qs = _stack_maps(q[:, sl], qk_dim)
            s = jnp.where(keep, scores(qs, kn[:, sl]), -jnp.inf)
            update(h, s, lambda p, sl=sl: weighted(p, vn[:, sl]))
            o = acc_scr[h] / l_scr[h]
            o_ref[:, sl] = o[:ts] - lam * o[ts:]


def _paged_call(page_table, lam, q, k_new, v_new, cache_k, cache_v, layer, heads, qk_dim, lam_init, ts):
    n, width = q.shape
    nb, n_pages = page_table.shape
    pages = _tile(n_pages, 8)
    n_groups = n_pages // pages
    page = cache_k.shape[-1]
    hw = 2 * qk_dim
    kern = functools.partial(_paged_kernel, pages=pages, heads=heads, qk_dim=qk_dim, lam_init=lam_init,
                             n_groups=n_groups)
    tok = pl.BlockSpec((ts, width), lambda b, g, pt: (b, 0))

    def k_spec(p):
        return pl.BlockSpec((1, 1, heads, 2, qk_dim, page),
                            lambda b, g, pt: (layer, pt[b, g * pages + p], 0, 0, 0, 0))

    def v_spec(p):
        return pl.BlockSpec((1, 1, page * heads, hw), lambda b, g, pt: (layer, pt[b, g * pages + p], 0, 0))

    grid_spec = pltpu.PrefetchScalarGridSpec(
        num_scalar_prefetch=1,
        grid=(nb, n_groups),
        in_specs=[pl.BlockSpec(lam.shape, lambda b, g, pt: (0, 0)), tok, tok, tok]
        + [k_spec(p) for p in range(pages)] + [v_spec(p) for p in range(pages)],
        out_specs=tok,
        scratch_shapes=[pltpu.VMEM((heads, 2 * ts, 1), F32), pltpu.VMEM((heads, 2 * ts, 1), F32),
                        pltpu.VMEM((heads, 2 * ts, hw), F32)],
    )
    return pl.pallas_call(
        kern,
        grid_spec=grid_spec,
        out_shape=jax.ShapeDtypeStruct((n, width), F32),
        compiler_params=_params("parallel", "arbitrary"),
        name="paged_attention",
    )(page_table, lam, q, k_new, v_new, *([cache_k] * pages), *([cache_v] * pages))


def _post_kernel(y_ref, r_ref, k_ref, v_ref, g_ref, o_ref, ga_ref, gb_ref, x_ref, g1_ref, sh2_ref, sc2_ref,
                 lng_ref, lnb_ref, rk_ref, sub_ref, n2_ref, ones_ref, wor_ref, wod_ref, wout_ref, *rest,
                 hs, v_dim, lam_init, n_experts, precise):
    if n_experts:
        router_ref, x_out_ref, h2_ref, gate_ref = rest
    else:
        x_out_ref, h2_ref = rest
    ones = ones_ref[...]
    y = y_ref[...]
    inv_hs = 1.0 / hs
    mu = _dot_exact_rhs(y, ones) * inv_hs
    yc = y - mu
    var = _dot_exact_rhs(yc * yc, ones) * inv_hs
    yn = yc * lax.rsqrt(var + LN_X_EPS) * lng_ref[...] + lnb_ref[...]
    v = v_ref[...]
    bonus = _dot_exact_rhs(r_ref[...] * k_ref[...] * rk_ref[...], ones) * v
    out_a = _mm((yn + bonus) * g_ref[...], wor_ref[...], precise)

    o = o_ref[...]
    parts = []
    for h in range(o.shape[1] // v_dim):
        oh = o[:, h * v_dim:(h + 1) * v_dim]
        parts.append(oh * lax.rsqrt(jnp.mean(oh * oh, axis=-1, keepdims=True) + NORM_EPS))
    on = jnp.concatenate(parts, axis=1) * sub_ref[...] * (1.0 - lam_init)
    out_b = _mm(on, wod_ref[...], precise)

    merged = _sigmoid(ga_ref[...]) * out_a + _sigmoid(gb_ref[...]) * out_b
    x = x_ref[...] + g1_ref[0] * _mm(merged, wout_ref[...], precise)
    x_out_ref[...] = x
    xn = x * lax.rsqrt(jnp.mean(x * x, axis=-1, keepdims=True) + NORM_EPS)
    h2 = (xn * n2_ref[...]) * (1.0 + sc2_ref[0]) + sh2_ref[0]
    h2_ref[...] = h2.astype(h2_ref.dtype)
    if n_experts:
        logits = _dot3(h2, router_ref[...])
        lane = _iota(logits.shape, 1).astype(F32)
        big = float(LANES)
        lg = jnp.where(lane < n_experts, logits, -jnp.inf)
        m1 = jnp.max(lg, axis=-1, keepdims=True)
        i1 = jnp.min(jnp.where(lg == m1, lane, big), axis=-1, keepdims=True)
        lg2 = jnp.where(lane == i1, -jnp.inf, lg)
        m2 = jnp.max(lg2, axis=-1, keepdims=True)
        i2 = jnp.min(jnp.where(lg2 == m2, lane, big), axis=-1, keepdims=True)
        e2 = jnp.exp(m2 - m1)
        den = 1.0 + e2
        gate_ref[...] = jnp.where(lane == i1, 1.0 / den, 0.0) + jnp.where(lane == i2, e2 / den, 0.0)


def _post_call(y, r, k, v, g, o, proj, x, rows, lw, ones_head, gate_cols, lam_init, hs, v_dim):
    n, d = x.shape
    width = y.shape[1]
    tm = _tile(n, 256)
    router = lw.get("router_pad")
    n_experts = lw["n_experts"] if router is not None else 0
    tok = lambda wd: pl.BlockSpec((tm, wd), lambda i: (i, 0))
    row_spec = lambda wd: pl.BlockSpec((1, wd), lambda i: (0, 0))
    full = lambda a: pl.BlockSpec(a.shape, lambda i: (0,) * a.ndim)
    in_specs = [tok(width)] * 6 + [
        pl.BlockSpec((tm, d), lambda i: (i, gate_cols[0])), pl.BlockSpec((tm, d), lambda i: (i, gate_cols[1])),
        tok(d), rows.mod_spec(tm, d, 2, 1), rows.mod_spec(tm, d, 3, 1), rows.mod_spec(tm, d, 4, 1),
        row_spec(width), row_spec(width), row_spec(width), row_spec(width), row_spec(d), full(ones_head),
        full(lw["w_o_rwkv"]), full(lw["w_o_diff"]), full(lw["w_out"]),
    ]
    args = [y, r, k, v, g, o, proj, proj, x, rows.mod3, rows.mod3, rows.mod3,
            lw["ln_x_g"], lw["ln_x_b"], lw["r_k"], lw["subln_g"], lw["norm2_g"], ones_head,
            lw["w_o_rwkv"], lw["w_o_diff"], lw["w_out"]]
    out_specs = [tok(d), tok(d)]
    precise = lw["w_out"].dtype == F32
    out_shape = [jax.ShapeDtypeStruct((n, d), F32), jax.ShapeDtypeStruct((n, d), F32 if precise else BF16)]
    if n_experts:
        in_specs.append(full(router))
        args.append(router)
        out_specs.append(tok(LANES))
        out_shape.append(jax.ShapeDtypeStruct((n, LANES), F32))
    kern = functools.partial(_post_kernel, hs=hs, v_dim=v_dim, lam_init=lam_init, n_experts=n_experts,
                             precise=precise)
    return pl.pallas_call(
        kern,
        grid=(n // tm,),
        in_specs=in_specs,
        out_specs=out_specs,
        out_shape=out_shape,
        compiler_params=_params("parallel"),
        name="merge_post",
    )(*args)


def _ffn_kernel(*refs, moe, precise):
    if moe:
        h_ref, x_ref, g2_ref, gate_ref, w1_ref, w3_ref, w2_ref, o_ref, acc_scr = refs
    else:
        h_ref, x_ref, g2_ref, w1_ref, w3_ref, w2_ref, o_ref, acc_scr = refs
    e = pl.program_id(1)
    f = pl.program_id(2)

    @pl.when((e == 0) & (f == 0))
    def _():
        acc_scr[...] = jnp.zeros(acc_scr.shape, F32)

    h = h_ref[...]
    a = _mm(h, w1_ref[0], precise)
    b = _mm(h, w3_ref[0], precise)
    act = a * _sigmoid(a) * b
    if moe:
        gate = gate_ref[...]
        lane = _iota(gate.shape, 1)
        act = act * jnp.sum(jnp.where(lane == e, gate, 0.0), axis=-1, keepdims=True)
    acc_scr[...] += _mm(act, w2_ref[0], precise)

    @pl.when((e == pl.num_programs(1) - 1) & (f == pl.num_programs(2) - 1))
    def _():
        o_ref[...] = x_ref[...] + g2_ref[0] * acc_scr[...]


def _ffn_call(h2, x, rows, gate, w1, w3, w2, tf):
    n, d = x.shape
    n_exp, _, ff = w1.shape
    tm = _tile(n, 512 if w1.dtype == BF16 else 256)
    moe = gate is not None
    tok = lambda wd: pl.BlockSpec((tm, wd), lambda i, e, f: (i, 0))
    in_specs = [tok(d), tok(d), rows.mod_spec(tm, d, 5, 3)]
    args = [h2, x, rows.mod3]
    if moe:
        in_specs.append(tok(LANES))
        args.append(gate)
    in_specs += [pl.BlockSpec((1, d, tf), lambda i, e, f: (e, 0, f)),
                 pl.BlockSpec((1, d, tf), lambda i, e, f: (e, 0, f)),
                 pl.BlockSpec((1, tf, d), lambda i, e, f: (e, f, 0))]
    args += [w1, w3, w2]
    return pl.pallas_call(
        functools.partial(_ffn_kernel, moe=moe, precise=w1.dtype == F32),
        grid=(n // tm, n_exp, ff // tf),
        in_specs=in_specs,
        out_specs=tok(d),
        out_shape=jax.ShapeDtypeStruct((n, d), F32),
        scratch_shapes=[pltpu.VMEM((tm, d), F32)],
        compiler_params=_params("parallel", "arbitrary", "arbitrary"),
        name="moe_ffn" if moe else "dense_ffn",
    )(*args)


def _ff_tile(ff):
    for t in (1408, 896, 512, 256, 128):
        if ff % t == 0 and ff // t >= 2:
            return t
    return ff


def _block_ones(width, seg):
    i = jnp.arange(width) // seg
    return (i[:, None] == i[None, :]).astype(BF16)


def _rope_tables(pos, qk_dim, n_seg):
    half = qk_dim // 2
    inv_freq = ROPE_THETA ** (-jnp.arange(half, dtype=F32) / half)
    ang = pos.astype(F32)[:, None] * inv_freq[None, :]
    cos, sin = jnp.cos(ang), jnp.sin(ang)
    cos_t = jnp.tile(jnp.concatenate([cos, cos], axis=1), (1, n_seg))
    sin_t = jnp.tile(jnp.concatenate([-sin, sin], axis=1), (1, n_seg))
    return cos_t, sin_t


def kernel(x_prompt, x_sample, c_prompt, c_sample, cache_k, cache_v, state_wkv, state_shift, page_table,
           norm1_g, norm2_g, w_ada, b_ada, w_in, mu_shift, w0, w_up, a0, a_up, g_up, k_k, k_a, r_k,
           ln_x_g, ln_x_b, w_o_rwkv, q_norm_g, k_norm_g, lam, subln_g, w_o_diff, w_out,
           ffn_w1, ffn_w3, ffn_w2, router, moe_w1, moe_w3, moe_w2):
    n_batch, seq, d = x_prompt.shape
    n_dec, dec_seq, _ = x_sample.shape
    depth = w_in.shape[0]
    heads_r, hs = r_k.shape[1], r_k.shape[2]
    width_r = heads_r * hs
    proj_r = mu_shift.shape[-1]
    n_phys, page, heads_d, _, qk_dim = cache_k.shape[1:]
    v_dim = cache_v.shape[-1]
    qk_w = heads_d * 2 * qk_dim
    v_w = heads_d * v_dim
    w_lora, a_lora = w_up.shape[1], a_up.shape[1]
    assert v_dim == 2 * qk_dim == LANES and qk_w == v_w and w_lora + a_lora == LANES
    past_len = page_table.shape[1] * page
    n_p, n_s = n_batch * seq, n_dec * dec_seq

    o_q = proj_r
    o_k, o_v, o_g = o_q + qk_w, o_q + 2 * qk_w, o_q + 2 * qk_w + v_w
    w_in_f = jnp.concatenate([w_in[:, :, o_g:], w_in[:, :, o_q:o_g], w_in[:, :, :o_q]], axis=-1)
    w_in_b = w_in_f.astype(BF16)
    n_gate = w_in.shape[-1] - o_g
    assert n_gate == 2 * d and (n_gate + 3 * qk_w) % proj_r == 0 and n_gate % qk_w == 0
    gate_cols = (0, 1)
    qkv_cols = tuple(n_gate // qk_w + j for j in range(3))
    rwkv_col = (n_gate + 3 * qk_w) // proj_r
    tn = proj_r

    c_all = jnp.concatenate([c_prompt, c_sample], axis=0)
    c_all = jnp.pad(c_all, ((0, (-c_all.shape[0]) % 8), (0, 0)))
    mod = _ada_call(c_all, w_ada, b_ada)

    ones_head = _block_ones(width_r, hs)
    ones_qk = _block_ones(qk_w, qk_dim)
    cos_p, sin_p = _rope_tables(jnp.arange(seq), qk_dim, qk_w // qk_dim)
    cos_s, sin_s = _rope_tables(past_len + jnp.arange(dec_seq), qk_dim, qk_w // qk_dim)
    cos_s, sin_s = jnp.tile(cos_s, (n_dec, 1)), jnp.tile(sin_s, (n_dec, 1))
    cache_kt = jnp.transpose(cache_k, (0, 1, 3, 4, 5, 2))
    cache_v4 = cache_v.reshape(depth, n_phys, page * heads_d, v_dim)

    zpad = lambda m, before, after: jnp.pad(m, ((before, after), (0, 0)))
    bf = lambda a: a.astype(BF16)
    xp = x_prompt.reshape(n_p, d)
    xs = x_sample.reshape(n_s, d)
    shift0 = jnp.zeros((n_batch, 1, proj_r), F32)
    wkv_zero = jnp.zeros((n_batch, heads_r, hs, hs), F32)
    wkv_chunk = _tile(seq, 64)
    dec_chunk = max(16, dec_seq)
    outs = {key: [] for key in ("kp", "vp", "ks", "vs", "wp", "ws", "sp", "ss")}

    for l in range(depth):
        lam_init = 0.8 - 0.6 * math.exp(-0.3 * l)
        lw = {
            "mu_shift": mu_shift[l][None], "w0": w0[l][None], "a0": a0[l][None],
            "w_up_pad": zpad(w_up[l], 0, a_lora), "a_up_pad": zpad(a_up[l], w_lora, 0), "g_up": g_up[l],
            "k_k": k_k[l][None], "k_a": k_a[l][None], "r_k": r_k[l].reshape(1, width_r),
            "ln_x_g": ln_x_g[l][None], "ln_x_b": ln_x_b[l][None],
            "subln_g": jnp.tile(subln_g[l], heads_d)[None], "norm2_g": norm2_g[l][None],
        }
        qg = jnp.tile(q_norm_g[l].reshape(-1), heads_d)[None]
        kg = jnp.tile(k_norm_g[l].reshape(-1), heads_d)[None]
        if l % 2 == 0:
            i = l // 2
            ffn_f = (ffn_w1[i][None], ffn_w3[i][None], ffn_w2[i][None])
        else:
            i = l // 2
            n_experts = router.shape[-1]
            lw["router_pad"] = jnp.pad(router[i], ((0, 0), (0, LANES - n_experts)))
            lw["n_experts"] = n_experts
            ffn_f = (moe_w1[i], moe_w3[i], moe_w2[i])
        tf = _ff_tile(ffn_f[0].shape[-1])
        norm1 = norm1_g[l][None]
        out_w = {"w_o_rwkv": w_o_rwkv[l], "w_o_diff": w_o_diff[l], "w_out": w_out[l]}
        lw_p = dict(lw, **{key: bf(val) for key, val in out_w.items()})
        lw_s = dict(lw, **out_w)
        ffn_b = tuple(bf(w) for w in ffn_f)

        def run_group(x, rows, init_rows, state0, is_prompt):
            lw_g, w_in_g, ffn = (lw_p, w_in_b[l], ffn_b) if is_prompt else (lw_s, w_in_f[l], ffn_f)
            proj = _proj_call(x, rows, norm1, w_in_g, tn)
            r, ld, k2, v_r, av, bv, g = _rwkv_prep_call(proj, rows, init_rows, lw_g, rwkv_col, ones_head,
                                                        not is_prompt)
            if is_prompt:
                y, state = _wkv_call(r, ld, k2, v_r, av, bv, state0, rows.seq_len, wkv_chunk, False)
                k_rows, v_rows, q_bf, k_bf, v_aug = _attn_prep_call(
                    proj, rows, cos_p, sin_p, qg, kg, ones_qk, qkv_cols, BF16, qk_dim)
                o = _flash_call(q_bf, k_bf, v_aug, lam[l], n_batch, seq, heads_d, qk_dim, lam_init)
            else:
                padt = lambda a: jnp.pad(a.reshape(n_dec, dec_seq, width_r),
                                         ((0, 0), (0, dec_chunk - dec_seq), (0, 0))).reshape(-1, width_r)
                y, state = _wkv_call(padt(r), padt(ld), padt(k2), padt(v_r), padt(av), padt(bv), state0,
                                     dec_chunk, dec_chunk, True)
                y = y.reshape(n_dec, dec_chunk, width_r)[:, :dec_seq].reshape(-1, width_r)
                k_rows, v_rows, q_f, _, _ = _attn_prep_call(
                    proj, rows, cos_s, sin_s, qg, kg, ones_qk, qkv_cols, F32, qk_dim)
                o = _paged_call(page_table, lam[l], q_f, k_rows, v_rows, cache_kt, cache_v4, l, heads_d,
                                qk_dim, lam_init, dec_seq)
            post = _post_call(y, r, k2, v_r, g, o, proj, x, rows, lw_g, ones_head, gate_cols, lam_init, hs, v_dim)
            x_new, h2 = post[0], post[1]
            gate = post[2] if len(post) > 2 else None
            x_out = _ffn_call(h2, x_new, rows, gate, *ffn, tf)
            shift = proj.reshape(-1, rows.seq_len, proj.shape[-1])[:, -1, rwkv_col * proj_r:]
            return x_out, k_rows, v_rows, state, shift

        rows_p = _Rows(n_p, seq, mod[l, :n_batch][:, None, :])
        rows_s = _Rows(n_s, dec_seq, jnp.repeat(mod[l, n_batch:n_batch + n_dec], dec_seq, axis=0)[None])
        xp, kr, vr, st, sh = run_group(xp, rows_p, shift0, wkv_zero, True)
        outs["kp"].append(kr.reshape(n_batch, seq, heads_d, 2, qk_dim))
        outs["vp"].append(vr.reshape(n_batch, seq, heads_d, v_dim))
        outs["wp"].append(st)
        outs["sp"].append(sh)
        init_s = jnp.repeat(state_shift[l], dec_seq, axis=0)
        xs, kr, vr, st, sh = run_group(xs, rows_s, init_s, state_wkv[l], False)
        outs["ks"].append(kr.reshape(n_dec, dec_seq, heads_d, 2, qk_dim))
        outs["vs"].append(vr.reshape(n_dec, dec_seq, heads_d, v_dim))
        outs["ws"].append(st)
        outs["ss"].append(sh)

    st = lambda key: jnp.stack(outs[key])
    return (xp.reshape(n_batch, seq, d), xs.reshape(n_dec, dec_seq, d), st("kp"), st("vp"), st("ks"), st("vs"),
            st("wp"), st("ws"), st("sp"), st("ss"))
```

```python
import functools
import math

import jax
import jax.numpy as jnp
from jax import lax
from jax.experimental import pallas as pl
from jax.experimental.pallas import tpu as pltpu

F32 = jnp.float32
BF16 = jnp.bfloat16

NORM_EPS = 1e-6
LN_X_EPS = 64e-5
ROPE_THETA = 10000.0
TOP_K = 2
KK_NORM_FLOOR = 1e-12
LANES = 128
VMEM_LIMIT = 56 * 1024 * 1024
FLASH_ROW_PARTS = 4


def _params(*sem):
    return pltpu.CompilerParams(dimension_semantics=sem, vmem_limit_bytes=VMEM_LIMIT)


def _tile(n, pref):
    if n <= pref:
        return n
    t = pref
    while n % t:
        t //= 2
    return t


def _bdot(a, b):
    return jnp.dot(a.astype(BF16), b.astype(BF16), preferred_element_type=F32)


def _mm(a, b, precise):
    return _dot3(a.astype(F32), b.astype(F32)) if precise else _bdot(a, b)


def _dot_nt(a, b):
    return lax.dot_general(a, b, (((1,), (1,)), ((), ())), preferred_element_type=F32)


def _split2(x):
    hi = x.astype(BF16)
    lo = (x - hi.astype(F32)).astype(BF16)
    return hi, lo


def _split3(x):
    hi = x.astype(BF16)
    r1 = x - hi.astype(F32)
    mid = r1.astype(BF16)
    lo = (r1 - mid.astype(F32)).astype(BF16)
    return hi, mid, lo


def _dot3(a, b):
    ah, al = _split2(a)
    bh, bl = _split2(b)
    d = functools.partial(jnp.dot, preferred_element_type=F32)
    return d(ah, bh) + d(ah, bl) + d(al, bh)


def _dot3_nt(a, b):
    ah, al = _split2(a)
    bh, bl = _split2(b)
    return _dot_nt(ah, bh) + _dot_nt(ah, bl) + _dot_nt(al, bh)


def _dot_exact_rhs(x, m_bf):
    hi, mid, lo = _split3(x)
    d = functools.partial(jnp.dot, preferred_element_type=F32)
    return d(hi, m_bf) + d(mid, m_bf) + d(lo, m_bf)


def _dot_exact_lhs(m_bf, x):
    hi, mid, lo = _split3(x)
    d = functools.partial(jnp.dot, preferred_element_type=F32)
    return d(m_bf, hi) + d(m_bf, mid) + d(m_bf, lo)


def _sigmoid(x):
    return 1.0 / (1.0 + jnp.exp(-x))


def _softplus(x):
    return jnp.maximum(x, 0.0) + jnp.log(1.0 + jnp.exp(-jnp.abs(x)))


def _iota(shape, dim):
    return lax.broadcasted_iota(jnp.int32, shape, dim)


def _ada_kernel(c_ref, w_ref, b_ref, o_ref):
    o_ref[0] = _dot3(c_ref[...], w_ref[0]) + b_ref[0]


def _ada_call(c_all, w_ada, b_ada):
    n_layers, d, d6 = w_ada.shape
    m = c_all.shape[0]
    tn = _tile(d6, 1536)
    return pl.pallas_call(
        _ada_kernel,
        grid=(n_layers, d6 // tn),
        in_specs=[
            pl.BlockSpec((m, d), lambda l, j: (0, 0)),
            pl.BlockSpec((1, d, tn), lambda l, j: (l, 0, j)),
            pl.BlockSpec((1, 1, tn), lambda l, j: (l, 0, j)),
        ],
        out_specs=pl.BlockSpec((1, m, tn), lambda l, j: (l, 0, j)),
        out_shape=jax.ShapeDtypeStruct((n_layers, m, d6), F32),
        compiler_params=_params("parallel", "parallel"),
        name="adaln_mod",
    )(c_all, w_ada, b_ada.reshape(n_layers, 1, d6))


class _Rows:
    def __init__(self, n, seq_len, mod3):
        self.n, self.seq_len, self.mod3 = n, seq_len, mod3
        self.per_row = mod3.shape[1] != 1

    def mod_spec(self, tm, d, col, n_grid):
        r = self.mod3.shape[1]
        if self.per_row:
            assert r == tm == self.n
            idx = lambda i, *_: (0, 0, col)
        else:
            assert self.seq_len % tm == 0
            per = self.seq_len // tm
            idx = lambda i, *_: (i // per, 0, col)
        return pl.BlockSpec((1, r, d), idx)


def _proj_kernel(x_ref, sh_ref, sc_ref, g_ref, w_ref, o_ref, h_scr, *, precise):
    @pl.when(pl.program_id(1) == 0)
    def _():
        x = x_ref[...]
        y = x * lax.rsqrt(jnp.mean(x * x, axis=-1, keepdims=True) + NORM_EPS)
        h = (y * g_ref[...]) * (1.0 + sc_ref[0]) + sh_ref[0]
        h_scr[...] = h.astype(h_scr.dtype)

    o_ref[...] = _mm(h_scr[...], w_ref[...], precise)


def _proj_call(x, rows, norm_g, w, tn):
    n, d = x.shape
    pw = w.shape[1]
    tm = _tile(n, 512)
    precise = w.dtype == F32
    return pl.pallas_call(
        functools.partial(_proj_kernel, precise=precise),
        grid=(n // tm, pw // tn),
        in_specs=[
            pl.BlockSpec((tm, d), lambda i, j: (i, 0)),
            rows.mod_spec(tm, d, 0, 2),
            rows.mod_spec(tm, d, 1, 2),
            pl.BlockSpec((1, d), lambda i, j: (0, 0)),
            pl.BlockSpec((d, tn), lambda i, j: (0, j)),
        ],
        out_specs=pl.BlockSpec((tm, tn), lambda i, j: (i, j)),
        out_shape=jax.ShapeDtypeStruct((n, pw), F32),
        scratch_shapes=[pltpu.VMEM((tm, d), w.dtype)],
        compiler_params=_params("parallel", "arbitrary"),
        name="proj_in",
    )(x, rows.mod3, rows.mod3, norm_g, w)


def _rwkv_prep_kernel(cols_ref, prev8_ref, init_ref, mu_ref, w0_ref, wup_ref, a0_ref, aup_ref,
                      gup_ref, kk_ref, ka_ref, ones_ref,
                      r_ref, ld_ref, k_ref, v_ref, av_ref, bv_ref, g_ref, *, seq_len, tm, width, precise):
    cols = cols_ref[...]
    rolled = pltpu.roll(cols, 1, 0)
    row = _iota((tm, 1), 0)
    if seq_len >= tm:
        first_tile = (pl.program_id(0) % (seq_len // tm)) == 0
        boundary = jnp.where(first_tile, init_ref[0], prev8_ref[7:8, :])
        prev = jnp.where(row == 0, boundary, rolled)
    else:
        prev = jnp.where(row % seq_len == 0, init_ref[...], rolled)
    xs = cols + (prev - cols) * mu_ref[...]
    w = width
    r = xs[:, 0:w]
    k = xs[:, w:2 * w]
    v = xs[:, 2 * w:3 * w]
    lora_wa = xs[:, 3 * w:3 * w + LANES]
    gd = xs[:, 3 * w + LANES:]
    w_log = -_softplus(-(w0_ref[...] + _mm(jnp.tanh(lora_wa), wup_ref[...], precise))) - 0.5
    a = _sigmoid(a0_ref[...] + _mm(lora_wa, aup_ref[...], precise))
    g = _mm(_sigmoid(gd), gup_ref[...], precise)
    kk = k * kk_ref[...]
    ss = _dot_exact_rhs(kk * kk, ones_ref[...])
    kk = kk / jnp.maximum(jnp.sqrt(ss), KK_NORM_FLOOR)
    r_ref[...] = r
    ld_ref[...] = -jnp.exp(w_log)
    k_ref[...] = k * (1.0 + (a - 1.0) * ka_ref[...])
    v_ref[...] = v
    av_ref[...] = -kk
    bv_ref[...] = kk * a
    g_ref[...] = g


def _rwkv_prep_call(proj, rows, init_rows, lw, col_block, ones_head, precise):
    n = proj.shape[0]
    width = lw["w0"].shape[-1]
    pw = lw["mu_shift"].shape[-1]
    tm = _tile(n, 256) if rows.seq_len >= 256 else n
    assert pw == 3 * width + LANES + lw["g_up"].shape[0]
    if rows.seq_len >= tm:
        init_spec = pl.BlockSpec((1, 1, pw), lambda i: ((i * tm) // rows.seq_len, 0, 0))
    else:
        init_spec = pl.BlockSpec((tm, pw), lambda i: (0, 0))
    row_spec = lambda wd: pl.BlockSpec((1, wd), lambda i: (0, 0))
    full = lambda a: pl.BlockSpec(a.shape, lambda i: (0,) * a.ndim)
    out = jax.ShapeDtypeStruct((n, width), F32)
    kern = functools.partial(_rwkv_prep_kernel, seq_len=rows.seq_len, tm=tm, width=width, precise=precise)
    return pl.pallas_call(
        kern,
        grid=(n // tm,),
        in_specs=[
            pl.BlockSpec((tm, pw), lambda i: (i, col_block)),
            pl.BlockSpec((8, pw), lambda i: (jnp.maximum(i * (tm // 8) - 1, 0), col_block)),
            init_spec,
            row_spec(pw), row_spec(width), full(lw["w_up_pad"]), row_spec(width), full(lw["a_up_pad"]),
            full(lw["g_up"]), row_spec(width), row_spec(width), full(ones_head),
        ],
        out_specs=[pl.BlockSpec((tm, width), lambda i: (i, 0))] * 7,
        out_shape=[out] * 7,
        compiler_params=_params("parallel"),
        name="rwkv_prep",
    )(proj, proj, init_rows, lw["mu_shift"], lw["w0"], lw["w_up_pad"], lw["a0"], lw["a_up_pad"],
      lw["g_up"], lw["k_k"], lw["k_a"], ones_head)


def _wkv_kernel(r_ref, ld_ref, k_ref, v_ref, a_ref, b_ref, s0_ref, y_ref, s_ref, *, chunk, heads, hs, precise):
    c = chunk
    if precise:
        mm, mm_nt = _dot3, _dot3_nt
    else:
        mm = _bdot
        mm_nt = lambda a, b: _dot_nt(a.astype(BF16), b.astype(BF16))

    @pl.when(pl.program_id(1) == 0)
    def _():
        s_ref[...] = s0_ref[...]

    rr = _iota((c, c), 0)
    cc = _iota((c, c), 1)
    strict = rr > cc
    incl = rr >= cc
    tri = incl.astype(BF16)
    eye_c = (rr == cc).astype(F32)
    eye_v = (_iota((hs, hs), 0) == _iota((hs, hs), 1)).astype(BF16)

    ld = ld_ref[...]
    cum = _dot_exact_lhs(tri, ld)
    tot = cum[c - 1:c, :]
    e_neg = jnp.exp(-cum)
    e_dec = jnp.exp(tot - cum)
    a_t = a_ref[...] * jnp.exp(cum - ld)
    r_t = r_ref[...] * jnp.exp(cum)
    b_t = b_ref[...] * e_neg
    k_t = k_ref[...] * e_neg
    b_d = b_ref[...] * e_dec
    k_d = k_ref[...] * e_dec
    e_tot = jnp.exp(tot)
    v_all = v_ref[...]

    n_double = int(math.log2(c)) - 1
    hr = range(heads)
    sls = [slice(h * hs, (h + 1) * hs) for h in hr]
    pm = [mm_nt(jnp.concatenate([a_t[:, sl], r_t[:, sl]], axis=0),
                jnp.concatenate([b_t[:, sl], k_t[:, sl]], axis=0)) for sl in sls]
    l_ab = [jnp.where(strict, p[:c, :c], 0.0) for p in pm]
    l_ak = [jnp.where(strict, p[:c, c:], 0.0) for p in pm]
    m_rb = [jnp.where(incl, p[c:, :c], 0.0) for p in pm]
    m_rk = [jnp.where(incl, p[c:, c:], 0.0) for p in pm]
    t_inv = [eye_c + l for l in l_ab]
    l_pow = l_ab
    for _ in range(n_double):
        l_pow = [mm(l, l) for l in l_pow]
        t_inv = [t + mm(t, l) for t, l in zip(t_inv, l_pow)]
    vh = [v_all[:, sl] for sl in sls]
    z = [mm(l_ak[h], vh[h]) for h in hr]
    au = [mm(t_inv[h], jnp.concatenate([a_t[:, sls[h]], z[h]], axis=1)) for h in hr]
    s0 = [s_ref[0, h] for h in hr]
    u = [mm_nt(au[h][:, :hs], s0[h]) + au[h][:, hs:] for h in hr]
    y = [mm_nt(r_t[:, sls[h]], s0[h]) + mm(m_rb[h], u[h]) + mm(m_rk[h], vh[h]) for h in hr]
    for h in hr:
        uv = jnp.concatenate([u[h], vh[h]], axis=0)
        if precise:
            uh, um, ul = _split3(uv)
            uv_t = _dot_nt(eye_v, uh) + _dot_nt(eye_v, um) + _dot_nt(eye_v, ul)
        else:
            uv_t = _dot_nt(eye_v, uv.astype(BF16))
        bkd = jnp.concatenate([b_d[:, sls[h]], k_d[:, sls[h]]], axis=0)
        s_ref[0, h] = s0[h] * e_tot[:, sls[h]] + mm(uv_t, bkd)
        y_ref[:, sls[h]] = y[h]


def _wkv_call(r, ld, k, v, av, bv, state0, seq_len, chunk, precise):
    n, width = r.shape
    nb, heads, hs, _ = state0.shape
    n_chunks = seq_len // chunk
    tok = pl.BlockSpec((chunk, width), lambda b, c: (b * n_chunks + c, 0))
    st = pl.BlockSpec((1, heads, hs, hs), lambda b, c: (b, 0, 0, 0))
    kern = functools.partial(_wkv_kernel, chunk=chunk, heads=heads, hs=hs, precise=precise)
    return pl.pallas_call(
        kern,
        grid=(nb, n_chunks),
        in_specs=[tok] * 6 + [st],
        out_specs=[tok, st],
        out_shape=[jax.ShapeDtypeStruct((n, width), F32), jax.ShapeDtypeStruct(state0.shape, F32)],
        compiler_params=_params("parallel", "arbitrary"),
        name="wkv_scan",
    )(r, ld, k, v, av, bv, state0)


def _attn_prep_kernel(q_ref, k_ref, v_ref, cos_ref, sin_ref, qg_ref, kg_ref, ones_ref,
                      kr_ref, vr_ref, qo_ref, kb_ref, vb_ref, *, qk_dim, scale):
    ones = ones_ref[...]
    cos = cos_ref[...]
    sin = sin_ref[...]
    lane = _iota(cos.shape, 1)
    first_half = (lane % qk_dim) < (qk_dim // 2)
    width = cos.shape[1]
    half = qk_dim // 2

    def norm_rope(x, g):
        ms = _dot_exact_rhs(x * x, ones) * (1.0 / qk_dim)
        y = x * lax.rsqrt(ms + NORM_EPS) * g
        swapped = jnp.where(first_half, pltpu.roll(y, width - half, 1), pltpu.roll(y, half, 1))
        return y * cos + swapped * sin

    q = norm_rope(q_ref[...], qg_ref[...])
    k = norm_rope(k_ref[...], kg_ref[...])
    v = v_ref[...]
    kr_ref[...] = k
    vr_ref[...] = v
    qo_ref[...] = (q * scale).astype(qo_ref.dtype)
    kb_ref[...] = k.astype(BF16)
    vb = v.astype(BF16)
    hw = 2 * qk_dim
    one = jnp.ones((vb.shape[0], hw), BF16)
    vb_ref[...] = jnp.concatenate(
        [blk for h in range(width // hw) for blk in (vb[:, h * hw:(h + 1) * hw], one)], axis=1)


def _attn_prep_call(proj, rows, cos, sin, qg, kg, ones_qk, col_blocks, q_dtype, qk_dim):
    n = proj.shape[0]
    width = qg.shape[-1]
    tm = _tile(n, 512) if rows.seq_len >= 512 else n
    per = max(rows.seq_len // tm, 1)
    tab = pl.BlockSpec((tm, width), lambda i: (i % per, 0))
    col = lambda cb: pl.BlockSpec((tm, width), lambda i: (i, cb))
    row_spec = pl.BlockSpec((1, width), lambda i: (0, 0))
    out_spec = pl.BlockSpec((tm, width), lambda i: (i, 0))
    kern = functools.partial(_attn_prep_kernel, qk_dim=qk_dim, scale=qk_dim ** -0.5)
    sds = lambda dt: jax.ShapeDtypeStruct((n, width), dt)
    return pl.pallas_call(
        kern,
        grid=(n // tm,),
        in_specs=[col(col_blocks[0]), col(col_blocks[1]), col(col_blocks[2]), tab, tab, row_spec, row_spec,
                  pl.BlockSpec(ones_qk.shape, lambda i: (0, 0))],
        out_specs=[out_spec] * 4 + [pl.BlockSpec((tm, 2 * width), lambda i: (i, 0))],
        out_shape=[sds(F32), sds(F32), sds(q_dtype), sds(BF16), jax.ShapeDtypeStruct((n, 2 * width), BF16)],
        compiler_params=_params("parallel"),
        name="attn_prep",
    )(proj, proj, proj, cos, sin, qg, kg, ones_qk)


def _lambda(lam_ref, lam_init):
    lp = lam_ref[...]
    s1 = jnp.sum(lp[0:1] * lp[1:2], axis=-1, keepdims=True)
    s2 = jnp.sum(lp[2:3] * lp[3:4], axis=-1, keepdims=True)
    return jnp.exp(s1) - jnp.exp(s2) + lam_init


def _stack_maps(qh, qk_dim):
    lane = _iota(qh.shape, 1)
    zero = jnp.zeros_like(qh)
    return jnp.concatenate([jnp.where(lane < qk_dim, qh, zero), jnp.where(lane >= qk_dim, qh, zero)], axis=0)


def _flash_kernel(lam_ref, q_ref, k_ref, v_ref, o_ref, qs_scr, m_scr, acc_scr, *, tq, tk, qk_dim, lam_init,
                  n_parts):
    i = pl.program_id(2)
    rows = 2 * tq
    hw = 2 * qk_dim
    n_blk = tk // hw
    qs_scr[...] = _stack_maps(q_ref[...], qk_dim)
    m_scr[...] = jnp.full(m_scr.shape, -jnp.inf, F32)
    acc_scr[...] = jnp.zeros(acc_scr.shape, F32)
    n_full = (i * tq) // tk

    part = rows // n_parts

    def step(j, masked):
        start = pl.multiple_of(j * tk, tk)
        kc = k_ref[pl.ds(start, tk), :]
        vc = v_ref[pl.ds(start, tk), :]
        for r0 in range(0, rows, part):
            rs = slice(r0, r0 + part)
            s = _dot_nt(qs_scr[rs, :], kc)
            if masked:
                qpos = i * tq + (r0 % tq) + _iota((part, tk), 0)
                kpos = start + _iota((part, tk), 1)
                s = jnp.where(kpos <= qpos, s, -jnp.inf)
            blocks = [s[:, c * hw:(c + 1) * hw] for c in range(n_blk)]
            blk_max = blocks[0]
            for blk in blocks[1:]:
                blk_max = jnp.maximum(blk_max, blk)
            m_prev = m_scr[rs, :]
            m_new = jnp.maximum(m_prev, jnp.max(blk_max, axis=-1, keepdims=True))
            alpha = jnp.exp(m_prev - m_new)
            p = jnp.concatenate([jnp.exp(blk - m_new).astype(BF16) for blk in blocks], axis=1)
            pv = jnp.dot(p, vc, preferred_element_type=F32)
            acc_scr[rs, :] = jnp.concatenate([alpha, alpha], axis=1) * acc_scr[rs, :] + pv
            m_scr[rs, :] = m_new

    def body(j, carry):
        step(j, False)
        return carry

    lax.fori_loop(0, n_full, body, 0)
    step(n_full, True)
    acc = acc_scr[...]
    o = acc[:, :hw] / acc[:, hw:]
    lam = _lambda(lam_ref, lam_init)
    o_ref[...] = o[:tq] - lam * o[tq:]


def _flash_call(q_bf, k_bf, v_aug, lam, n_batch, seq_len, heads, qk_dim, lam_init):
    n, width = q_bf.shape
    hw = width // heads
    tq = _tile(seq_len, 512)
    tk = _tile(seq_len, 1024)
    nq = seq_len // tq
    kern = functools.partial(_flash_kernel, tq=tq, tk=tk, qk_dim=qk_dim, lam_init=lam_init,
                             n_parts=FLASH_ROW_PARTS)
    qspec = pl.BlockSpec((tq, hw), lambda b, h, i: (b * nq + i, h))
    return pl.pallas_call(
        kern,
        grid=(n_batch, heads, nq),
        in_specs=[pl.BlockSpec(lam.shape, lambda b, h, i: (0, 0)), qspec,
                  pl.BlockSpec((seq_len, hw), lambda b, h, i: (b, h)),
                  pl.BlockSpec((seq_len, 2 * hw), lambda b, h, i: (b, h))],
        out_specs=qspec,
        out_shape=jax.ShapeDtypeStruct((n, width), F32),
        scratch_shapes=[pltpu.VMEM((2 * tq, hw), BF16), pltpu.VMEM((2 * tq, hw), F32),
                        pltpu.VMEM((2 * tq, 2 * hw), F32)],
        compiler_params=_params("parallel", "parallel", "arbitrary"),
        name="prompt_attention",
    )(lam, q_bf, k_bf, v_aug)


def _paged_kernel(pt_ref, lam_ref, q_ref, kn_ref, vn_ref, *refs, pages, heads, qk_dim, lam_init, n_groups):
    del pt_ref
    k_refs = refs[:pages]
    v_refs = refs[pages:2 * pages]
    o_ref = refs[2 * pages]
    m_scr, l_scr, acc_scr = refs[2 * pages + 1:]
    g = pl.program_id(1)
    hw = 2 * qk_dim
    ts = q_ref.shape[0]

    @pl.when(g == 0)
    def _():
        m_scr[...] = jnp.full(m_scr.shape, -jnp.inf, F32)
        l_scr[...] = jnp.zeros(l_scr.shape, F32)
        acc_scr[...] = jnp.zeros(acc_scr.shape, F32)

    q = q_ref[...]

    def update(h, s, pv_fn):
        m_prev = m_scr[h]
        m_new = jnp.maximum(m_prev, jnp.max(s, axis=-1, keepdims=True))
        alpha = jnp.exp(m_prev - m_new)
        p = jnp.exp(s - m_new)
        l_scr[h] = alpha * l_scr[h] + jnp.sum(p, axis=-1, keepdims=True)
        acc_scr[h] = alpha * acc_scr[h] + pv_fn(p)
        m_scr[h] = m_new

    def scores(qs, keys):
        qh, ql = _split2(qs)
        kh, kl = _split2(keys)
        both = _dot_nt(jnp.concatenate([qh, ql], axis=0), kh)
        return both[:2 * ts] + both[2 * ts:] + _dot_nt(qh, kl)

    def weighted(p, vals):
        ph, plo = _split2(p)
        vh, vl = _split2(vals)
        both = jnp.dot(jnp.concatenate([ph, plo], axis=0), vh, preferred_element_type=F32)
        return both[:2 * ts] + both[2 * ts:] + jnp.dot(ph, vl, preferred_element_type=F32)

    def scores_t(qs, keys_t):
        qh, ql = _split2(qs)
        kh, kl = _split2(keys_t)
        both = jnp.dot(jnp.concatenate([qh, ql], axis=0), kh, preferred_element_type=F32)
        return both[:2 * ts] + both[2 * ts:] + jnp.dot(qh, kl, preferred_element_type=F32)

    page = k_refs[0].shape[-1]
    for h in range(heads):
        sl = slice(h * hw, (h + 1) * hw)
        qs = _stack_maps(q[:, sl], qk_dim)
        s = jnp.concatenate([scores_t(qs, k_refs[p][0, 0, h].reshape(hw, page)) for p in range(pages)], axis=1)

        def pv(p_all, h=h):
            acc = None
            for p in range(pages):
                vals = v_refs[p][0, 0, pl.ds(h, page, stride=heads), :]
                t = weighted(p_all[:, p * page:(p + 1) * page], vals)
                acc = t if acc is None else acc + t
            return acc

        update(h, s, pv)

    @pl.when(g == n_groups - 1)
    def _():
        lam = _lambda(lam_ref, lam_init)
        pad = jnp.zeros((LANES - ts, kn_ref.shape[1]), F32)
        kn = jnp.concatenate([kn_ref[...], pad], axis=0)
        vn = jnp.concatenate([vn_ref[...], pad], axis=0)
        qpos = _iota((2 * ts, LANES), 0) % ts
        kpos = _iota((2 * ts, LANES), 1)
        keep = kpos <= qpos
        for h in range(heads):
            sl = slice(h * hw, (h + 1) * hw)
            qs = _stack_maps(q[:, sl], qk_dim)
            s = jnp.where(keep, scores(qs, kn[:, sl]), -jnp.inf)
            update(h, s, lambda p, sl=sl: weighted(p, vn[:, sl]))
            o = acc_scr[h] / l_scr[h]
            o_ref[:, sl] = o[:ts] - lam * o[ts:]


def _paged_call(page_table, lam, q, k_new, v_new, cache_k, cache_v, layer, heads, qk_dim, lam_init, ts):
    n, width = q.shape
    nb, n_pages = page_table.shape
    pages = _tile(n_pages, 8)
    n_groups = n_pages // pages
    page = cache_k.shape[-1]
    hw = 2 * qk_dim
    kern = functools.partial(_paged_kernel, pages=pages, heads=heads, qk_dim=qk_dim, lam_init=lam_init,
                             n_groups=n_groups)
    tok = pl.BlockSpec((ts, width), lambda b, g, pt: (b, 0))

    def k_spec(p):
        return pl.BlockSpec((1, 1, heads, 2, qk_dim, page),
                            lambda b, g, pt: (layer, pt[b, g * pages + p], 0, 0, 0, 0))

    def v_spec(p):
        return pl.BlockSpec((1, 1, page * heads, hw), lambda b, g, pt: (layer, pt[b, g * pages + p], 0, 0))

    grid_spec = pltpu.PrefetchScalarGridSpec(
        num_scalar_prefetch=1,
        grid=(nb, n_groups),
        in_specs=[pl.BlockSpec(lam.shape, lambda b, g, pt: (0, 0)), tok, tok, tok]
        + [k_spec(p) for p in range(pages)] + [v_spec(p) for p in range(pages)],
        out_specs=tok,
        scratch_shapes=[pltpu.VMEM((heads, 2 * ts, 1), F32), pltpu.VMEM((heads, 2 * ts, 1), F32),
                        pltpu.VMEM((heads, 2 * ts, hw), F32)],
    )
    return pl.pallas_call(
        kern,
        grid_spec=grid_spec,
        out_shape=jax.ShapeDtypeStruct((n, width), F32),
        compiler_params=_params("parallel", "arbitrary"),
        name="paged_attention",
    )(page_table, lam, q, k_new, v_new, *([cache_k] * pages), *([cache_v] * pages))


def _post_kernel(y_ref, r_ref, k_ref, v_ref, g_ref, o_ref, ga_ref, gb_ref, x_ref, g1_ref, sh2_ref, sc2_ref,
                 lng_ref, lnb_ref, rk_ref, sub_ref, n2_ref, ones_ref, wor_ref, wod_ref, wout_ref, *rest,
                 hs, v_dim, lam_init, n_experts, precise):
    if n_experts:
        router_ref, x_out_ref, h2_ref, gate_ref = rest
    else:
        x_out_ref, h2_ref = rest
    ones = ones_ref[...]
    y = y_ref[...]
    inv_hs = 1.0 / hs
    mu = _dot_exact_rhs(y, ones) * inv_hs
    yc = y - mu
    var = _dot_exact_rhs(yc * yc, ones) * inv_hs
    yn = yc * lax.rsqrt(var + LN_X_EPS) * lng_ref[...] + lnb_ref[...]
    v = v_ref[...]
    bonus = _dot_exact_rhs(r_ref[...] * k_ref[...] * rk_ref[...], ones) * v
    out_a = _mm((yn + bonus) * g_ref[...], wor_ref[...], precise)

    o = o_ref[...]
    parts = []
    for h in range(o.shape[1] // v_dim):
        oh = o[:, h * v_dim:(h + 1) * v_dim]
        parts.append(oh * lax.rsqrt(jnp.mean(oh * oh, axis=-1, keepdims=True) + NORM_EPS))
    on = jnp.concatenate(parts, axis=1) * sub_ref[...] * (1.0 - lam_init)
    out_b = _mm(on, wod_ref[...], precise)

    merged = _sigmoid(ga_ref[...]) * out_a + _sigmoid(gb_ref[...]) * out_b
    x = x_ref[...] + g1_ref[0] * _mm(merged, wout_ref[...], precise)
    x_out_ref[...] = x
    xn = x * lax.rsqrt(jnp.mean(x * x, axis=-1, keepdims=True) + NORM_EPS)
    h2 = (xn * n2_ref[...]) * (1.0 + sc2_ref[0]) + sh2_ref[0]
    h2_ref[...] = h2.astype(h2_ref.dtype)
    if n_experts:
        logits = _dot3(h2, router_ref[...])
        lane = _iota(logits.shape, 1).astype(F32)
        big = float(LANES)
        lg = jnp.where(lane < n_experts, logits, -jnp.inf)
        m1 = jnp.max(lg, axis=-1, keepdims=True)
        i1 = jnp.min(jnp.where(lg == m1, lane, big), axis=-1, keepdims=True)
        lg2 = jnp.where(lane == i1, -jnp.inf, lg)
        m2 = jnp.max(lg2, axis=-1, keepdims=True)
        i2 = jnp.min(jnp.where(lg2 == m2, lane, big), axis=-1, keepdims=True)
        e2 = jnp.exp(m2 - m1)
        den = 1.0 + e2
        gate_ref[...] = jnp.where(lane == i1, 1.0 / den, 0.0) + jnp.where(lane == i2, e2 / den, 0.0)


def _post_call(y, r, k, v, g, o, proj, x, rows, lw, ones_head, gate_cols, lam_init, hs, v_dim):
    n, d = x.shape
    width = y.shape[1]
    tm = _tile(n, 256)
    router = lw.get("router_pad")
    n_experts = lw["n_experts"] if router is not None else 0
    tok = lambda wd: pl.BlockSpec((tm, wd), lambda i: (i, 0))
    row_spec = lambda wd: pl.BlockSpec((1, wd), lambda i: (0, 0))
    full = lambda a: pl.BlockSpec(a.shape, lambda i: (0,) * a.ndim)
    in_specs = [tok(width)] * 6 + [
        pl.BlockSpec((tm, d), lambda i: (i, gate_cols[0])), pl.BlockSpec((tm, d), lambda i: (i, gate_cols[1])),
        tok(d), rows.mod_spec(tm, d, 2, 1), rows.mod_spec(tm, d, 3, 1), rows.mod_spec(tm, d, 4, 1),
        row_spec(width), row_spec(width), row_spec(width), row_spec(width), row_spec(d), full(ones_head),
        full(lw["w_o_rwkv"]), full(lw["w_o_diff"]), full(lw["w_out"]),
    ]
    args = [y, r, k, v, g, o, proj, proj, x, rows.mod3, rows.mod3, rows.mod3,
            lw["ln_x_g"], lw["ln_x_b"], lw["r_k"], lw["subln_g"], lw["norm2_g"], ones_head,
            lw["w_o_rwkv"], lw["w_o_diff"], lw["w_out"]]
    out_specs = [tok(d), tok(d)]
    precise = lw["w_out"].dtype == F32
    out_shape = [jax.ShapeDtypeStruct((n, d), F32), jax.ShapeDtypeStruct((n, d), F32 if precise else BF16)]
    if n_experts:
        in_specs.append(full(router))
        args.append(router)
        out_specs.append(tok(LANES))
        out_shape.append(jax.ShapeDtypeStruct((n, LANES), F32))
    kern = functools.partial(_post_kernel, hs=hs, v_dim=v_dim, lam_init=lam_init, n_experts=n_experts,
                             precise=precise)
    return pl.pallas_call(
        kern,
        grid=(n // tm,),
        in_specs=in_specs,
        out_specs=out_specs,
        out_shape=out_shape,
        compiler_params=_params("parallel"),
        name="merge_post",
    )(*args)


def _ffn_kernel(*refs, moe, precise):
    if moe:
        h_ref, x_ref, g2_ref, gate_ref, w1_ref, w3_ref, w2_ref, o_ref, acc_scr = refs
    else:
        h_ref, x_ref, g2_ref, w1_ref, w3_ref, w2_ref, o_ref, acc_scr = refs
    e = pl.program_id(1)
    f = pl.program_id(2)

    @pl.when((e == 0) & (f == 0))
    def _():
        acc_scr[...] = jnp.zeros(acc_scr.shape, F32)

    h = h_ref[...]
    a = _mm(h, w1_ref[0], precise)
    b = _mm(h, w3_ref[0], precise)
    act = a * _sigmoid(a) * b
    if moe:
        gate = gate_ref[...]
        lane = _iota(gate.shape, 1)
        act = act * jnp.sum(jnp.where(lane == e, gate, 0.0), axis=-1, keepdims=True)
    acc_scr[...] += _mm(act, w2_ref[0], precise)

    @pl.when((e == pl.num_programs(1) - 1) & (f == pl.num_programs(2) - 1))
    def _():
        o_ref[...] = x_ref[...] + g2_ref[0] * acc_scr[...]


def _ffn_call(h2, x, rows, gate, w1, w3, w2, tf):
    n, d = x.shape
    n_exp, _, ff = w1.shape
    tm = _tile(n, 512 if w1.dtype == BF16 else 256)
    moe = gate is not None
    tok = lambda wd: pl.BlockSpec((tm, wd), lambda i, e, f: (i, 0))
    in_specs = [tok(d), tok(d), rows.mod_spec(tm, d, 5, 3)]
    args = [h2, x, rows.mod3]
    if moe:
        in_specs.append(tok(LANES))
        args.append(gate)
    in_specs += [pl.BlockSpec((1, d, tf), lambda i, e, f: (e, 0, f)),
                 pl.BlockSpec((1, d, tf), lambda i, e, f: (e, 0, f)),
                 pl.BlockSpec((1, tf, d), lambda i, e, f: (e, f, 0))]
    args += [w1, w3, w2]
    return pl.pallas_call(
        functools.partial(_ffn_kernel, moe=moe, precise=w1.dtype == F32),
        grid=(n // tm, n_exp, ff // tf),
        in_specs=in_specs,
        out_specs=tok(d),
        out_shape=jax.ShapeDtypeStruct((n, d), F32),
        scratch_shapes=[pltpu.VMEM((tm, d), F32)],
        compiler_params=_params("parallel", "arbitrary", "arbitrary"),
        name="moe_ffn" if moe else "dense_ffn",
    )(*args)


def _ff_tile(ff):
    for t in (1408, 896, 512, 256, 128):
        if ff % t == 0 and ff // t >= 2:
            return t
    return ff


MOE_ROW_BLOCK = 128


def _moe_routed_kernel(nblk_ref, h_ref, x_ref, g2_ref, gate_ref, tri_ref, w1_ref, w3_ref, w2_ref, o_ref,
                       acc_scr, slotc_scr, slotr_scr, hsel_scr, ysel_scr, gsel_scr, *, rb, n_exp):
    i = pl.program_id(0)
    e = pl.program_id(1)
    f = pl.program_id(2)
    last_f = pl.num_programs(2) - 1
    tm = h_ref.shape[0]
    nb = nblk_ref[i * n_exp + e]
    dot = functools.partial(jnp.dot, preferred_element_type=F32)

    @pl.when((e == 0) & (f == 0))
    def _():
        acc_scr[...] = jnp.zeros(acc_scr.shape, F32)
        routed = gate_ref[...] > 0.0
        ind = jnp.where(routed, 1.0, 0.0).astype(BF16)
        tri = tri_ref[...]
        rank = dot(tri, ind)
        slotc_scr[...] = jnp.where(routed, rank, -1.0)
        eye = (_iota((LANES, LANES), 0) == _iota((LANES, LANES), 1)).astype(BF16)
        ind_t = _dot_nt(eye, ind)
        rank_t = _dot_nt(ind_t.astype(BF16), tri)
        slotr_scr[...] = jnp.where(ind_t > 0.5, rank_t, -1.0)

    @pl.when(f == 0)
    def _():
        slot_row = slotr_scr[pl.ds(e, 1), :]

        def select(b, carry):
            r0 = pl.multiple_of(b * rb, rb)
            want = (r0 + _iota((rb, 1), 0)).astype(F32)
            onehot = jnp.where(slot_row == want, 1.0, 0.0).astype(BF16)
            hsel_scr[pl.ds(r0, rb), :] = dot(onehot, h_ref[...]).astype(BF16)
            g = _dot_exact_lhs(onehot, gate_ref[...])
            ge = jnp.sum(jnp.where(_iota(g.shape, 1) == e, g, 0.0), axis=-1, keepdims=True)
            gsel_scr[pl.ds(r0, rb), :] = jnp.broadcast_to(ge, (rb, LANES))
            ysel_scr[pl.ds(r0, rb), :] = jnp.zeros((rb, ysel_scr.shape[1]), F32)
            return carry

        lax.fori_loop(0, nb, select, 0)

    def expert(b, carry):
        r0 = pl.multiple_of(b * rb, rb)
        hs = hsel_scr[pl.ds(r0, rb), :]
        a = dot(hs, w1_ref[0])
        act = a * _sigmoid(a) * dot(hs, w3_ref[0]) * gsel_scr[pl.ds(r0, rb), 0:1]
        ysel_scr[pl.ds(r0, rb), :] += dot(act.astype(BF16), w2_ref[0])
        return carry

    lax.fori_loop(0, nb, expert, 0)

    @pl.when(f == last_f)
    def _():
        slotc = slotc_scr[...]
        slot_col = jnp.sum(jnp.where(_iota(slotc.shape, 1) == e, slotc, 0.0), axis=-1, keepdims=True)

        def scatter(b, carry):
            r0 = pl.multiple_of(b * rb, rb)
            want = (r0 + _iota((1, rb), 1)).astype(F32)
            onehot_t = jnp.where(slot_col == want, 1.0, 0.0).astype(BF16)
            yh, yl = _split2(ysel_scr[pl.ds(r0, rb), :])
            acc_scr[...] += dot(onehot_t, yh) + dot(onehot_t, yl)
            return carry

        lax.fori_loop(0, nb, scatter, 0)

    @pl.when((e == n_exp - 1) & (f == last_f))
    def _():
        o_ref[...] = x_ref[...] + g2_ref[0] * acc_scr[...]


def _moe_routed_call(h2, x, rows, gate, w1, w3, w2, tf):
    n, d = x.shape
    n_exp, _, ff = w1.shape
    rb = MOE_ROW_BLOCK
    tm = _tile(n, 1024)
    n_tiles = n // tm
    counts = jnp.sum((gate[:, :n_exp] > 0.0).reshape(n_tiles, tm, n_exp).astype(jnp.int32), axis=1)
    nblk = ((counts + rb - 1) // rb).reshape(-1)
    t = jnp.arange(tm)
    tri = (t[None, :] < t[:, None]).astype(BF16)
    tok = lambda wd: pl.BlockSpec((tm, wd), lambda i, e, f, nb: (i, 0))
    mod = rows.mod_spec(tm, d, 5, 3)
    grid_spec = pltpu.PrefetchScalarGridSpec(
        num_scalar_prefetch=1,
        grid=(n_tiles, n_exp, ff // tf),
        in_specs=[tok(d), tok(d), pl.BlockSpec(mod.block_shape, lambda i, e, f, nb: mod.index_map(i)), tok(LANES),
                  pl.BlockSpec((tm, tm), lambda i, e, f, nb: (0, 0)),
                  pl.BlockSpec((1, d, tf), lambda i, e, f, nb: (e, 0, f)),
                  pl.BlockSpec((1, d, tf), lambda i, e, f, nb: (e, 0, f)),
                  pl.BlockSpec((1, tf, d), lambda i, e, f, nb: (e, f, 0))],
        out_specs=tok(d),
        scratch_shapes=[pltpu.VMEM((tm, d), F32), pltpu.VMEM((tm, LANES), F32), pltpu.VMEM((LANES, tm), F32),
                        pltpu.VMEM((tm, d), BF16), pltpu.VMEM((tm, d), F32), pltpu.VMEM((tm, LANES), F32)],
    )
    return pl.pallas_call(
        functools.partial(_moe_routed_kernel, rb=rb, n_exp=n_exp),
        grid_spec=grid_spec,
        out_shape=jax.ShapeDtypeStruct((n, d), F32),
        compiler_params=_params("parallel", "arbitrary", "arbitrary"),
        name="moe_routed",
    )(nblk, h2, x, rows.mod3, gate, tri, w1, w3, w2)


def _block_ones(width, seg):
    i = jnp.arange(width) // seg
    return (i[:, None] == i[None, :]).astype(BF16)


def _rope_tables(pos, qk_dim, n_seg):
    half = qk_dim // 2
    inv_freq = ROPE_THETA ** (-jnp.arange(half, dtype=F32) / half)
    ang = pos.astype(F32)[:, None] * inv_freq[None, :]
    cos, sin = jnp.cos(ang), jnp.sin(ang)
    cos_t = jnp.tile(jnp.concatenate([cos, cos], axis=1), (1, n_seg))
    sin_t = jnp.tile(jnp.concatenate([-sin, sin], axis=1), (1, n_seg))
    return cos_t, sin_t


def kernel(x_prompt, x_sample, c_prompt, c_sample, cache_k, cache_v, state_wkv, state_shift, page_table,
           norm1_g, norm2_g, w_ada, b_ada, w_in, mu_shift, w0, w_up, a0, a_up, g_up, k_k, k_a, r_k,
           ln_x_g, ln_x_b, w_o_rwkv, q_norm_g, k_norm_g, lam, subln_g, w_o_diff, w_out,
           ffn_w1, ffn_w3, ffn_w2, router, moe_w1, moe_w3, moe_w2):
    n_batch, seq, d = x_prompt.shape
    n_dec, dec_seq, _ = x_sample.shape
    depth = w_in.shape[0]
    heads_r, hs = r_k.shape[1], r_k.shape[2]
    width_r = heads_r * hs
    proj_r = mu_shift.shape[-1]
    n_phys, page, heads_d, _, qk_dim = cache_k.shape[1:]
    v_dim = cache_v.shape[-1]
    qk_w = heads_d * 2 * qk_dim
    v_w = heads_d * v_dim
    w_lora, a_lora = w_up.shape[1], a_up.shape[1]
    assert v_dim == 2 * qk_dim == LANES and qk_w == v_w and w_lora + a_lora == LANES
    past_len = page_table.shape[1] * page
    n_p, n_s = n_batch * seq, n_dec * dec_seq

    o_q = proj_r
    o_k, o_v, o_g = o_q + qk_w, o_q + 2 * qk_w, o_q + 2 * qk_w + v_w
    w_in_f = jnp.concatenate([w_in[:, :, o_g:], w_in[:, :, o_q:o_g], w_in[:, :, :o_q]], axis=-1)
    w_in_b = w_in_f.astype(BF16)
    n_gate = w_in.shape[-1] - o_g
    assert n_gate == 2 * d and (n_gate + 3 * qk_w) % proj_r == 0 and n_gate % qk_w == 0
    gate_cols = (0, 1)
    qkv_cols = tuple(n_gate // qk_w + j for j in range(3))
    rwkv_col = (n_gate + 3 * qk_w) // proj_r
    tn = proj_r

    c_all = jnp.concatenate([c_prompt, c_sample], axis=0)
    c_all = jnp.pad(c_all, ((0, (-c_all.shape[0]) % 8), (0, 0)))
    mod = _ada_call(c_all, w_ada, b_ada)

    ones_head = _block_ones(width_r, hs)
    ones_qk = _block_ones(qk_w, qk_dim)
    cos_p, sin_p = _rope_tables(jnp.arange(seq), qk_dim, qk_w // qk_dim)
    cos_s, sin_s = _rope_tables(past_len + jnp.arange(dec_seq), qk_dim, qk_w // qk_dim)
    cos_s, sin_s = jnp.tile(cos_s, (n_dec, 1)), jnp.tile(sin_s, (n_dec, 1))
    cache_kt = jnp.transpose(cache_k, (0, 1, 3, 4, 5, 2))
    cache_v4 = cache_v.reshape(depth, n_phys, page * heads_d, v_dim)

    zpad = lambda m, before, after: jnp.pad(m, ((before, after), (0, 0)))
    bf = lambda a: a.astype(BF16)
    xp = x_prompt.reshape(n_p, d)
    xs = x_sample.reshape(n_s, d)
    shift0 = jnp.zeros((n_batch, 1, proj_r), F32)
    wkv_zero = jnp.zeros((n_batch, heads_r, hs, hs), F32)
    wkv_chunk = _tile(seq, 64)
    dec_chunk = max(16, dec_seq)
    outs = {key: [] for key in ("kp", "vp", "ks", "vs", "wp", "ws", "sp", "ss")}

    for l in range(depth):
        lam_init = 0.8 - 0.6 * math.exp(-0.3 * l)
        lw = {
            "mu_shift": mu_shift[l][None], "w0": w0[l][None], "a0": a0[l][None],
            "w_up_pad": zpad(w_up[l], 0, a_lora), "a_up_pad": zpad(a_up[l], w_lora, 0), "g_up": g_up[l],
            "k_k": k_k[l][None], "k_a": k_a[l][None], "r_k": r_k[l].reshape(1, width_r),
            "ln_x_g": ln_x_g[l][None], "ln_x_b": ln_x_b[l][None],
            "subln_g": jnp.tile(subln_g[l], heads_d)[None], "norm2_g": norm2_g[l][None],
        }
        qg = jnp.tile(q_norm_g[l].reshape(-1), heads_d)[None]
        kg = jnp.tile(k_norm_g[l].reshape(-1), heads_d)[None]
        if l % 2 == 0:
            i = l // 2
            ffn_f = (ffn_w1[i][None], ffn_w3[i][None], ffn_w2[i][None])
        else:
            i = l // 2
            n_experts = router.shape[-1]
            lw["router_pad"] = jnp.pad(router[i], ((0, 0), (0, LANES - n_experts)))
            lw["n_experts"] = n_experts
            ffn_f = (moe_w1[i], moe_w3[i], moe_w2[i])
        tf = _ff_tile(ffn_f[0].shape[-1])
        norm1 = norm1_g[l][None]
        out_w = {"w_o_rwkv": w_o_rwkv[l], "w_o_diff": w_o_diff[l], "w_out": w_out[l]}
        lw_p = dict(lw, **{key: bf(val) for key, val in out_w.items()})
        lw_s = dict(lw, **out_w)
        ffn_b = tuple(bf(w) for w in ffn_f)

        def run_group(x, rows, init_rows, state0, is_prompt):
            lw_g, w_in_g, ffn = (lw_p, w_in_b[l], ffn_b) if is_prompt else (lw_s, w_in_f[l], ffn_f)
            proj = _proj_call(x, rows, norm1, w_in_g, tn)
            r, ld, k2, v_r, av, bv, g = _rwkv_prep_call(proj, rows, init_rows, lw_g, rwkv_col, ones_head,
                                                        not is_prompt)
            if is_prompt:
                y, state = _wkv_call(r, ld, k2, v_r, av, bv, state0, rows.seq_len, wkv_chunk, False)
                k_rows, v_rows, q_bf, k_bf, v_aug = _attn_prep_call(
                    proj, rows, cos_p, sin_p, qg, kg, ones_qk, qkv_cols, BF16, qk_dim)
                o = _flash_call(q_bf, k_bf, v_aug, lam[l], n_batch, seq, heads_d, qk_dim, lam_init)
            else:
                padt = lambda a: jnp.pad(a.reshape(n_dec, dec_seq, width_r),
                                         ((0, 0), (0, dec_chunk - dec_seq), (0, 0))).reshape(-1, width_r)
                y, state = _wkv_call(padt(r), padt(ld), padt(k2), padt(v_r), padt(av), padt(bv), state0,
                                     dec_chunk, dec_chunk, True)
                y = y.reshape(n_dec, dec_chunk, width_r)[:, :dec_seq].reshape(-1, width_r)
                k_rows, v_rows, q_f, _, _ = _attn_prep_call(
                    proj, rows, cos_s, sin_s, qg, kg, ones_qk, qkv_cols, F32, qk_dim)
                o = _paged_call(page_table, lam[l], q_f, k_rows, v_rows, cache_kt, cache_v4, l, heads_d,
                                qk_dim, lam_init, dec_seq)
            post = _post_call(y, r, k2, v_r, g, o, proj, x, rows, lw_g, ones_head, gate_cols, lam_init, hs, v_dim)
            x_new, h2 = post[0], post[1]
            gate = post[2] if len(post) > 2 else None
            ffn_call = _moe_routed_call if (is_prompt and gate is not None) else _ffn_call
            x_out = ffn_call(h2, x_new, rows, gate, *ffn, tf)
            shift = proj.reshape(-1, rows.seq_len, proj.shape[-1])[:, -1, rwkv_col * proj_r:]
            return x_out, k_rows, v_rows, state, shift

        rows_p = _Rows(n_p, seq, mod[l, :n_batch][:, None, :])
        rows_s = _Rows(n_s, dec_seq, jnp.repeat(mod[l, n_batch:n_batch + n_dec], dec_seq, axis=0)[None])
        xp, kr, vr, st, sh = run_group(xp, rows_p, shift0, wkv_zero, True)
        outs["kp"].append(kr.reshape(n_batch, seq, heads_d, 2, qk_dim))
        outs["vp"].append(vr.reshape(n_batch, seq, heads_d, v_dim))
        outs["wp"].append(st)
        outs["sp"].append(sh)
        init_s = jnp.repeat(state_shift[l], dec_seq, axis=0)
        xs, kr, vr, st, sh = run_group(xs, rows_s, init_s, state_wkv[l], False)
        outs["ks"].append(kr.reshape(n_dec, dec_seq, heads_d, 2, qk_dim))
        outs["vs"].append(vr.reshape(n_dec, dec_seq, heads_d, v_dim))
        outs["ws"].append(st)
        outs["ss"].append(sh)

    st = lambda key: jnp.stack(outs[key])
    return (xp.reshape(n_batch, seq, d), xs.reshape(n_dec, dec_seq, d), st("kp"), st("vp"), st("ks"), st("vs"),
            st("wp"), st("ws"), st("sp"), st("ss"))
```

```python
import functools
import math

import jax
import jax.numpy as jnp
from jax import lax
from jax.experimental import pallas as pl
from jax.experimental.pallas import tpu as pltpu

F32 = jnp.float32
BF16 = jnp.bfloat16

NORM_EPS = 1e-6
LN_X_EPS = 64e-5
ROPE_THETA = 10000.0
TOP_K = 2
KK_NORM_FLOOR = 1e-12
LANES = 128
VMEM_LIMIT = 56 * 1024 * 1024
FLASH_ROW_PARTS = 4
WKV_CHUNKS_PER_STEP = 4


def _params(*sem):
    return pltpu.CompilerParams(dimension_semantics=sem, vmem_limit_bytes=VMEM_LIMIT)


def _tile(n, pref):
    if n <= pref:
        return n
    t = pref
    while n % t:
        t //= 2
    return t


def _bdot(a, b):
    return jnp.dot(a.astype(BF16), b.astype(BF16), preferred_element_type=F32)


def _mm(a, b, precise):
    return _dot3(a.astype(F32), b.astype(F32)) if precise else _bdot(a, b)


def _dot_nt(a, b):
    return lax.dot_general(a, b, (((1,), (1,)), ((), ())), preferred_element_type=F32)


def _split2(x):
    hi = x.astype(BF16)
    lo = (x - hi.astype(F32)).astype(BF16)
    return hi, lo


def _split3(x):
    hi = x.astype(BF16)
    r1 = x - hi.astype(F32)
    mid = r1.astype(BF16)
    lo = (r1 - mid.astype(F32)).astype(BF16)
    return hi, mid, lo


def _dot3(a, b):
    ah, al = _split2(a)
    bh, bl = _split2(b)
    d = functools.partial(jnp.dot, preferred_element_type=F32)
    return d(ah, bh) + d(ah, bl) + d(al, bh)


def _dot3_nt(a, b):
    ah, al = _split2(a)
    bh, bl = _split2(b)
    return _dot_nt(ah, bh) + _dot_nt(ah, bl) + _dot_nt(al, bh)


def _dot_exact_rhs(x, m_bf):
    hi, mid, lo = _split3(x)
    d = functools.partial(jnp.dot, preferred_element_type=F32)
    return d(hi, m_bf) + d(mid, m_bf) + d(lo, m_bf)


def _dot_exact_lhs(m_bf, x):
    hi, mid, lo = _split3(x)
    d = functools.partial(jnp.dot, preferred_element_type=F32)
    return d(m_bf, hi) + d(m_bf, mid) + d(m_bf, lo)


def _sigmoid(x):
    return 1.0 / (1.0 + jnp.exp(-x))


def _softplus(x):
    return jnp.maximum(x, 0.0) + jnp.log(1.0 + jnp.exp(-jnp.abs(x)))


def _iota(shape, dim):
    return lax.broadcasted_iota(jnp.int32, shape, dim)


def _ada_kernel(c_ref, w_ref, b_ref, o_ref):
    o_ref[0] = _dot3(c_ref[...], w_ref[0]) + b_ref[0]


def _ada_call(c_all, w_ada, b_ada):
    n_layers, d, d6 = w_ada.shape
    m = c_all.shape[0]
    tn = _tile(d6, 1536)
    return pl.pallas_call(
        _ada_kernel,
        grid=(n_layers, d6 // tn),
        in_specs=[
            pl.BlockSpec((m, d), lambda l, j: (0, 0)),
            pl.BlockSpec((1, d, tn), lambda l, j: (l, 0, j)),
            pl.BlockSpec((1, 1, tn), lambda l, j: (l, 0, j)),
        ],
        out_specs=pl.BlockSpec((1, m, tn), lambda l, j: (l, 0, j)),
        out_shape=jax.ShapeDtypeStruct((n_layers, m, d6), F32),
        compiler_params=_params("parallel", "parallel"),
        name="adaln_mod",
    )(c_all, w_ada, b_ada.reshape(n_layers, 1, d6))


class _Rows:
    def __init__(self, n, seq_len, mod3):
        self.n, self.seq_len, self.mod3 = n, seq_len, mod3
        self.per_row = mod3.shape[1] != 1

    def mod_spec(self, tm, d, col, n_grid):
        r = self.mod3.shape[1]
        if self.per_row:
            assert r == tm == self.n
            idx = lambda i, *_: (0, 0, col)
        else:
            assert self.seq_len % tm == 0
            per = self.seq_len // tm
            idx = lambda i, *_: (i // per, 0, col)
        return pl.BlockSpec((1, r, d), idx)


def _proj_kernel(x_ref, sh_ref, sc_ref, g_ref, w_ref, o_ref, h_scr, *, precise):
    @pl.when(pl.program_id(1) == 0)
    def _():
        x = x_ref[...]
        y = x * lax.rsqrt(jnp.mean(x * x, axis=-1, keepdims=True) + NORM_EPS)
        h = (y * g_ref[...]) * (1.0 + sc_ref[0]) + sh_ref[0]
        h_scr[...] = h.astype(h_scr.dtype)

    o_ref[...] = _mm(h_scr[...], w_ref[...], precise)


def _proj_call(x, rows, norm_g, w, tn):
    n, d = x.shape
    pw = w.shape[1]
    tm = _tile(n, 512)
    precise = w.dtype == F32
    return pl.pallas_call(
        functools.partial(_proj_kernel, precise=precise),
        grid=(n // tm, pw // tn),
        in_specs=[
            pl.BlockSpec((tm, d), lambda i, j: (i, 0)),
            rows.mod_spec(tm, d, 0, 2),
            rows.mod_spec(tm, d, 1, 2),
            pl.BlockSpec((1, d), lambda i, j: (0, 0)),
            pl.BlockSpec((d, tn), lambda i, j: (0, j)),
        ],
        out_specs=pl.BlockSpec((tm, tn), lambda i, j: (i, j)),
        out_shape=jax.ShapeDtypeStruct((n, pw), F32),
        scratch_shapes=[pltpu.VMEM((tm, d), w.dtype)],
        compiler_params=_params("parallel", "arbitrary"),
        name="proj_in",
    )(x, rows.mod3, rows.mod3, norm_g, w)


def _rwkv_prep_kernel(cols_ref, prev8_ref, init_ref, mu_ref, w0_ref, wup_ref, a0_ref, aup_ref,
                      gup_ref, kk_ref, ka_ref, ones_ref,
                      r_ref, ld_ref, k_ref, v_ref, av_ref, bv_ref, g_ref, *, seq_len, tm, width, precise):
    cols = cols_ref[...]
    rolled = pltpu.roll(cols, 1, 0)
    row = _iota((tm, 1), 0)
    if seq_len >= tm:
        first_tile = (pl.program_id(0) % (seq_len // tm)) == 0
        boundary = jnp.where(first_tile, init_ref[0], prev8_ref[7:8, :])
        prev = jnp.where(row == 0, boundary, rolled)
    else:
        prev = jnp.where(row % seq_len == 0, init_ref[...], rolled)
    xs = cols + (prev - cols) * mu_ref[...]
    w = width
    r = xs[:, 0:w]
    k = xs[:, w:2 * w]
    v = xs[:, 2 * w:3 * w]
    lora_wa = xs[:, 3 * w:3 * w + LANES]
    gd = xs[:, 3 * w + LANES:]
    w_log = -_softplus(-(w0_ref[...] + _mm(jnp.tanh(lora_wa), wup_ref[...], precise))) - 0.5
    a = _sigmoid(a0_ref[...] + _mm(lora_wa, aup_ref[...], precise))
    g = _mm(_sigmoid(gd), gup_ref[...], precise)
    kk = k * kk_ref[...]
    ss = _dot_exact_rhs(kk * kk, ones_ref[...])
    kk = kk / jnp.maximum(jnp.sqrt(ss), KK_NORM_FLOOR)
    r_ref[...] = r
    ld_ref[...] = -jnp.exp(w_log)
    k_ref[...] = k * (1.0 + (a - 1.0) * ka_ref[...])
    v_ref[...] = v
    av_ref[...] = -kk
    bv_ref[...] = kk * a
    g_ref[...] = g


def _rwkv_prep_call(proj, rows, init_rows, lw, col_block, ones_head, precise):
    n = proj.shape[0]
    width = lw["w0"].shape[-1]
    pw = lw["mu_shift"].shape[-1]
    tm = _tile(n, 256) if rows.seq_len >= 256 else n
    assert pw == 3 * width + LANES + lw["g_up"].shape[0]
    if rows.seq_len >= tm:
        init_spec = pl.BlockSpec((1, 1, pw), lambda i: ((i * tm) // rows.seq_len, 0, 0))
    else:
        init_spec = pl.BlockSpec((tm, pw), lambda i: (0, 0))
    row_spec = lambda wd: pl.BlockSpec((1, wd), lambda i: (0, 0))
    full = lambda a: pl.BlockSpec(a.shape, lambda i: (0,) * a.ndim)
    out = jax.ShapeDtypeStruct((n, width), F32)
    kern = functools.partial(_rwkv_prep_kernel, seq_len=rows.seq_len, tm=tm, width=width, precise=precise)
    return pl.pallas_call(
        kern,
        grid=(n // tm,),
        in_specs=[
            pl.BlockSpec((tm, pw), lambda i: (i, col_block)),
            pl.BlockSpec((8, pw), lambda i: (jnp.maximum(i * (tm // 8) - 1, 0), col_block)),
            init_spec,
            row_spec(pw), row_spec(width), full(lw["w_up_pad"]), row_spec(width), full(lw["a_up_pad"]),
            full(lw["g_up"]), row_spec(width), row_spec(width), full(ones_head),
        ],
        out_specs=[pl.BlockSpec((tm, width), lambda i: (i, 0))] * 7,
        out_shape=[out] * 7,
        compiler_params=_params("parallel"),
        name="rwkv_prep",
    )(proj, proj, init_rows, lw["mu_shift"], lw["w0"], lw["w_up_pad"], lw["a0"], lw["a_up_pad"],
      lw["g_up"], lw["k_k"], lw["k_a"], ones_head)


def _wkv_kernel(r_ref, ld_ref, k_ref, v_ref, a_ref, b_ref, s0_ref, y_ref, s_ref, *, chunk, heads, hs, precise):
    c = chunk
    if precise:
        mm, mm_nt = _dot3, _dot3_nt
    else:
        mm = _bdot
        mm_nt = lambda a, b: _dot_nt(a.astype(BF16), b.astype(BF16))

    @pl.when(pl.program_id(1) == 0)
    def _():
        s_ref[...] = s0_ref[...]

    rr = _iota((c, c), 0)
    cc = _iota((c, c), 1)
    strict = rr > cc
    incl = rr >= cc
    tri = incl.astype(BF16)
    eye_c = (rr == cc).astype(F32)
    eye_v = (_iota((hs, hs), 0) == _iota((hs, hs), 1)).astype(BF16)

    n_double = int(math.log2(c)) - 1
    hr = range(heads)
    sls = [slice(h * hs, (h + 1) * hs) for h in hr]
    n_sub = r_ref.shape[0] // c
    items = [(ci, h) for ci in range(n_sub) for h in hr]

    pre = []
    for ci in range(n_sub):
        rows = slice(ci * c, (ci + 1) * c)
        ld = ld_ref[rows, :]
        cum = _dot_exact_lhs(tri, ld)
        tot = cum[c - 1:c, :]
        e_neg = jnp.exp(-cum)
        e_dec = jnp.exp(tot - cum)
        b_all, k_all = b_ref[rows, :], k_ref[rows, :]
        pre.append(dict(a_t=a_ref[rows, :] * jnp.exp(cum - ld), r_t=r_ref[rows, :] * jnp.exp(cum),
                        b_t=b_all * e_neg, k_t=k_all * e_neg, b_d=b_all * e_dec, k_d=k_all * e_dec,
                        e_tot=jnp.exp(tot), v=v_ref[rows, :]))
    get = lambda name, ci, h: pre[ci][name][:, sls[h]]

    pm = [mm_nt(jnp.concatenate([get("a_t", ci, h), get("r_t", ci, h)], axis=0),
                jnp.concatenate([get("b_t", ci, h), get("k_t", ci, h)], axis=0)) for ci, h in items]
    l_ab = [jnp.where(strict, p[:c, :c], 0.0) for p in pm]
    l_ak = [jnp.where(strict, p[:c, c:], 0.0) for p in pm]
    m_rb = [jnp.where(incl, p[c:, :c], 0.0) for p in pm]
    m_rk = [jnp.where(incl, p[c:, c:], 0.0) for p in pm]
    t_inv = [eye_c + l for l in l_ab]
    l_pow = l_ab
    for _ in range(n_double):
        l_pow = [mm(l, l) for l in l_pow]
        t_inv = [t + mm(t, l) for t, l in zip(t_inv, l_pow)]
    vh = [get("v", ci, h) for ci, h in items]
    z = [mm(l_ak[j], vh[j]) for j in range(len(items))]
    au = [mm(t_inv[j], jnp.concatenate([get("a_t", ci, h), z[j]], axis=1)) for j, (ci, h) in enumerate(items)]
    state = [s_ref[0, h] for h in hr]
    for ci in range(n_sub):
        rows = slice(ci * c, (ci + 1) * c)
        for h in hr:
            j = ci * heads + h
            s0 = state[h]
            u = mm_nt(au[j][:, :hs], s0) + au[j][:, hs:]
            y = mm_nt(get("r_t", ci, h), s0) + mm(m_rb[j], u) + mm(m_rk[j], vh[j])
            uv = jnp.concatenate([u, vh[j]], axis=0)
            if precise:
                uh, um, ul = _split3(uv)
                uv_t = _dot_nt(eye_v, uh) + _dot_nt(eye_v, um) + _dot_nt(eye_v, ul)
            else:
                uv_t = _dot_nt(eye_v, uv.astype(BF16))
            bkd = jnp.concatenate([get("b_d", ci, h), get("k_d", ci, h)], axis=0)
            state[h] = s0 * pre[ci]["e_tot"][:, sls[h]] + mm(uv_t, bkd)
            y_ref[rows, sls[h]] = y
    for h in hr:
        s_ref[0, h] = state[h]


def _wkv_call(r, ld, k, v, av, bv, state0, seq_len, chunk, precise, n_sub=1):
    n, width = r.shape
    nb, heads, hs, _ = state0.shape
    n_chunks = seq_len // (chunk * n_sub)
    tok = pl.BlockSpec((chunk * n_sub, width), lambda b, c: (b * n_chunks + c, 0))
    st = pl.BlockSpec((1, heads, hs, hs), lambda b, c: (b, 0, 0, 0))
    kern = functools.partial(_wkv_kernel, chunk=chunk, heads=heads, hs=hs, precise=precise)
    return pl.pallas_call(
        kern,
        grid=(nb, n_chunks),
        in_specs=[tok] * 6 + [st],
        out_specs=[tok, st],
        out_shape=[jax.ShapeDtypeStruct((n, width), F32), jax.ShapeDtypeStruct(state0.shape, F32)],
        compiler_params=_params("parallel", "arbitrary"),
        name="wkv_scan",
    )(r, ld, k, v, av, bv, state0)


def _attn_prep_kernel(q_ref, k_ref, v_ref, cos_ref, sin_ref, qg_ref, kg_ref, ones_ref,
                      kr_ref, vr_ref, qo_ref, kb_ref, vb_ref, *, qk_dim, scale):
    ones = ones_ref[...]
    cos = cos_ref[...]
    sin = sin_ref[...]
    lane = _iota(cos.shape, 1)
    first_half = (lane % qk_dim) < (qk_dim // 2)
    width = cos.shape[1]
    half = qk_dim // 2

    def norm_rope(x, g):
        ms = _dot_exact_rhs(x * x, ones) * (1.0 / qk_dim)
        y = x * lax.rsqrt(ms + NORM_EPS) * g
        swapped = jnp.where(first_half, pltpu.roll(y, width - half, 1), pltpu.roll(y, half, 1))
        return y * cos + swapped * sin

    q = norm_rope(q_ref[...], qg_ref[...])
    k = norm_rope(k_ref[...], kg_ref[...])
    v = v_ref[...]
    kr_ref[...] = k
    vr_ref[...] = v
    qo_ref[...] = (q * scale).astype(qo_ref.dtype)
    kb_ref[...] = k.astype(BF16)
    vb = v.astype(BF16)
    hw = 2 * qk_dim
    one = jnp.ones((vb.shape[0], hw), BF16)
    vb_ref[...] = jnp.concatenate(
        [blk for h in range(width // hw) for blk in (vb[:, h * hw:(h + 1) * hw], one)], axis=1)


def _attn_prep_call(proj, rows, cos, sin, qg, kg, ones_qk, col_blocks, q_dtype, qk_dim):
    n = proj.shape[0]
    width = qg.shape[-1]
    tm = _tile(n, 512) if rows.seq_len >= 512 else n
    per = max(rows.seq_len // tm, 1)
    tab = pl.BlockSpec((tm, width), lambda i: (i % per, 0))
    col = lambda cb: pl.BlockSpec((tm, width), lambda i: (i, cb))
    row_spec = pl.BlockSpec((1, width), lambda i: (0, 0))
    out_spec = pl.BlockSpec((tm, width), lambda i: (i, 0))
    kern = functools.partial(_attn_prep_kernel, qk_dim=qk_dim, scale=qk_dim ** -0.5)
    sds = lambda dt: jax.ShapeDtypeStruct((n, width), dt)
    return pl.pallas_call(
        kern,
        grid=(n // tm,),
        in_specs=[col(col_blocks[0]), col(col_blocks[1]), col(col_blocks[2]), tab, tab, row_spec, row_spec,
                  pl.BlockSpec(ones_qk.shape, lambda i: (0, 0))],
        out_specs=[out_spec] * 4 + [pl.BlockSpec((tm, 2 * width), lambda i: (i, 0))],
        out_shape=[sds(F32), sds(F32), sds(q_dtype), sds(BF16), jax.ShapeDtypeStruct((n, 2 * width), BF16)],
        compiler_params=_params("parallel"),
        name="attn_prep",
    )(proj, proj, proj, cos, sin, qg, kg, ones_qk)


def _lambda(lam_ref, lam_init):
    lp = lam_ref[...]
    s1 = jnp.sum(lp[0:1] * lp[1:2], axis=-1, keepdims=True)
    s2 = jnp.sum(lp[2:3] * lp[3:4], axis=-1, keepdims=True)
    return jnp.exp(s1) - jnp.exp(s2) + lam_init


def _stack_maps(qh, qk_dim):
    lane = _iota(qh.shape, 1)
    zero = jnp.zeros_like(qh)
    return jnp.concatenate([jnp.where(lane < qk_dim, qh, zero), jnp.where(lane >= qk_dim, qh, zero)], axis=0)


def _flash_kernel(lam_ref, q_ref, k_ref, v_ref, o_ref, qs_scr, m_scr, acc_scr, *, tq, tk, qk_dim, lam_init,
                  n_parts):
    i = pl.program_id(2)
    rows = 2 * tq
    hw = 2 * qk_dim
    n_blk = tk // hw
    qs_scr[...] = _stack_maps(q_ref[...], qk_dim)
    m_scr[...] = jnp.full(m_scr.shape, -jnp.inf, F32)
    acc_scr[...] = jnp.zeros(acc_scr.shape, F32)
    n_full = (i * tq) // tk

    part = rows // n_parts

    def step(j, masked):
        start = pl.multiple_of(j * tk, tk)
        kc = k_ref[pl.ds(start, tk), :]
        vc = v_ref[pl.ds(start, tk), :]
        for r0 in range(0, rows, part):
            rs = slice(r0, r0 + part)
            s = _dot_nt(qs_scr[rs, :], kc)
            if masked:
                qpos = i * tq + (r0 % tq) + _iota((part, tk), 0)
                kpos = start + _iota((part, tk), 1)
                s = jnp.where(kpos <= qpos, s, -jnp.inf)
            blocks = [s[:, c * hw:(c + 1) * hw] for c in range(n_blk)]
            blk_max = blocks[0]
            for blk in blocks[1:]:
                blk_max = jnp.maximum(blk_max, blk)
            m_prev = m_scr[rs, :]
            m_new = jnp.maximum(m_prev, jnp.max(blk_max, axis=-1, keepdims=True))
            alpha = jnp.exp(m_prev - m_new)
            p = jnp.concatenate([jnp.exp(blk - m_new).astype(BF16) for blk in blocks], axis=1)
            pv = jnp.dot(p, vc, preferred_element_type=F32)
            acc_scr[rs, :] = jnp.concatenate([alpha, alpha], axis=1) * acc_scr[rs, :] + pv
            m_scr[rs, :] = m_new

    def body(j, carry):
        step(j, False)
        return carry

    lax.fori_loop(0, n_full, body, 0)
    step(n_full, True)
    acc = acc_scr[...]
    o = acc[:, :hw] / acc[:, hw:]
    lam = _lambda(lam_ref, lam_init)
    o_ref[...] = o[:tq] - lam * o[tq:]


def _flash_call(q_bf, k_bf, v_aug, lam, n_batch, seq_len, heads, qk_dim, lam_init):
    n, width = q_bf.shape
    hw = width // heads
    tq = _tile(seq_len, 512)
    tk = _tile(seq_len, 1024)
    nq = seq_len // tq
    kern = functools.partial(_flash_kernel, tq=tq, tk=tk, qk_dim=qk_dim, lam_init=lam_init,
                             n_parts=FLASH_ROW_PARTS)
    qspec = pl.BlockSpec((tq, hw), lambda b, h, i: (b * nq + i, h))
    return pl.pallas_call(
        kern,
        grid=(n_batch, heads, nq),
        in_specs=[pl.BlockSpec(lam.shape, lambda b, h, i: (0, 0)), qspec,
                  pl.BlockSpec((seq_len, hw), lambda b, h, i: (b, h)),
                  pl.BlockSpec((seq_len, 2 * hw), lambda b, h, i: (b, h))],
        out_specs=qspec,
        out_shape=jax.ShapeDtypeStruct((n, width), F32),
        scratch_shapes=[pltpu.VMEM((2 * tq, hw), BF16), pltpu.VMEM((2 * tq, hw), F32),
                        pltpu.VMEM((2 * tq, 2 * hw), F32)],
        compiler_params=_params("parallel", "parallel", "arbitrary"),
        name="prompt_attention",
    )(lam, q_bf, k_bf, v_aug)


def _paged_kernel(pt_ref, lam_ref, q_ref, kn_ref, vn_ref, *refs, pages, heads, qk_dim, lam_init, n_groups):
    del pt_ref
    k_refs = refs[:pages]
    v_refs = refs[pages:2 * pages]
    o_ref = refs[2 * pages]
    m_scr, l_scr, acc_scr = refs[2 * pages + 1:]
    g = pl.program_id(1)
    hw = 2 * qk_dim
    ts = q_ref.shape[0]

    @pl.when(g == 0)
    def _():
        m_scr[...] = jnp.full(m_scr.shape, -jnp.inf, F32)
        l_scr[...] = jnp.zeros(l_scr.shape, F32)
        acc_scr[...] = jnp.zeros(acc_scr.shape, F32)

    q = q_ref[...]

    def update(h, s, pv_fn):
        m_prev = m_scr[h]
        m_new = jnp.maximum(m_prev, jnp.max(s, axis=-1, keepdims=True))
        alpha = jnp.exp(m_prev - m_new)
        p = jnp.exp(s - m_new)
        l_scr[h] = alpha * l_scr[h] + jnp.sum(p, axis=-1, keepdims=True)
        acc_scr[h] = alpha * acc_scr[h] + pv_fn(p)
        m_scr[h] = m_new

    def scores(qs, keys):
        qh, ql = _split2(qs)
        kh, kl = _split2(keys)
        both = _dot_nt(jnp.concatenate([qh, ql], axis=0), kh)
        return both[:2 * ts] + both[2 * ts:] + _dot_nt(qh, kl)

    def weighted(p, vals):
        ph, plo = _split2(p)
        vh, vl = _split2(vals)
        both = jnp.dot(jnp.concatenate([ph, plo], axis=0), vh, preferred_element_type=F32)
        return both[:2 * ts] + both[2 * ts:] + jnp.dot(ph, vl, preferred_element_type=F32)

    def scores_t(qs, keys_t):
        qh, ql = _split2(qs)
        kh, kl = _split2(keys_t)
        both = jnp.dot(jnp.concatenate([qh, ql], axis=0), kh, preferred_element_type=F32)
        return both[:2 * ts] + both[2 * ts:] + jnp.dot(qh, kl, preferred_element_type=F32)

    page = k_refs[0].shape[-1]
    hr = range(heads)
    qs = [_stack_maps(q[:, h * hw:(h + 1) * hw], qk_dim) for h in hr]
    keys_t = [jnp.concatenate([k_refs[p][0, 0, h].reshape(hw, page) for p in range(pages)], axis=1) for h in hr]
    s = [scores_t(qs[h], keys_t[h]) for h in hr]
    m_prev = [m_scr[h] for h in hr]
    m_new = [jnp.maximum(m_prev[h], jnp.max(s[h], axis=-1, keepdims=True)) for h in hr]
    alpha = [jnp.exp(m_prev[h] - m_new[h]) for h in hr]
    prob = [jnp.exp(s[h] - m_new[h]) for h in hr]
    vals = [jnp.concatenate([v_refs[p][0, 0, pl.ds(h, page, stride=heads), :] for p in range(pages)], axis=0)
            for h in hr]
    pv = [weighted(prob[h], vals[h]) for h in hr]
    for h in hr:
        l_scr[h] = alpha[h] * l_scr[h] + jnp.sum(prob[h], axis=-1, keepdims=True)
        acc_scr[h] = alpha[h] * acc_scr[h] + pv[h]
        m_scr[h] = m_new[h]

    @pl.when(g == n_groups - 1)
    def _():
        lam = _lambda(lam_ref, lam_init)
        pad = jnp.zeros((LANES - ts, kn_ref.shape[1]), F32)
        kn = jnp.concatenate([kn_ref[...], pad], axis=0)
        vn = jnp.concatenate([vn_ref[...], pad], axis=0)
        qpos = _iota((2 * ts, LANES), 0) % ts
        kpos = _iota((2 * ts, LANES), 1)
        keep = kpos <= qpos
        for h in range(heads):
            sl = slice(h * hw, (h + 1) * hw)
            qs = _stack_maps(q[:, sl], qk_dim)
            s = jnp.where(keep, scores(qs, kn[:, sl]), -jnp.inf)
            update(h, s, lambda p, sl=sl: weighted(p, vn[:, sl]))
            o = acc_scr[h] / l_scr[h]
            o_ref[:, sl] = o[:ts] - lam * o[ts:]


def _paged_call(page_table, lam, q, k_new, v_new, cache_k, cache_v, layer, heads, qk_dim, lam_init, ts):
    n, width = q.shape
    nb, n_pages = page_table.shape
    pages = _tile(n_pages, 8)
    n_groups = n_pages // pages
    page = cache_k.shape[-1]
    hw = 2 * qk_dim
    kern = functools.partial(_paged_kernel, pages=pages, heads=heads, qk_dim=qk_dim, lam_init=lam_init,
                             n_groups=n_groups)
    tok = pl.BlockSpec((ts, width), lambda b, g, pt: (b, 0))

    def k_spec(p):
        return pl.BlockSpec((1, 1, heads, 2, qk_dim, page),
                            lambda b, g, pt: (layer, pt[b, g * pages + p], 0, 0, 0, 0))

    def v_spec(p):
        return pl.BlockSpec((1, 1, page * heads, hw), lambda b, g, pt: (layer, pt[b, g * pages + p], 0, 0))

    grid_spec = pltpu.PrefetchScalarGridSpec(
        num_scalar_prefetch=1,
        grid=(nb, n_groups),
        in_specs=[pl.BlockSpec(lam.shape, lambda b, g, pt: (0, 0)), tok, tok, tok]
        + [k_spec(p) for p in range(pages)] + [v_spec(p) for p in range(pages)],
        out_specs=tok,
        scratch_shapes=[pltpu.VMEM((heads, 2 * ts, 1), F32), pltpu.VMEM((heads, 2 * ts, 1), F32),
                        pltpu.VMEM((heads, 2 * ts, hw), F32)],
    )
    return pl.pallas_call(
        kern,
        grid_spec=grid_spec,
        out_shape=jax.ShapeDtypeStruct((n, width), F32),
        compiler_params=_params("parallel", "arbitrary"),
        name="paged_attention",
    )(page_table, lam, q, k_new, v_new, *([cache_k] * pages), *([cache_v] * pages))


def _post_kernel(y_ref, r_ref, k_ref, v_ref, g_ref, o_ref, ga_ref, gb_ref, x_ref, g1_ref, sh2_ref, sc2_ref,
                 lng_ref, lnb_ref, rk_ref, sub_ref, n2_ref, ones_ref, wor_ref, wod_ref, wout_ref, *rest,
                 hs, v_dim, lam_init, n_experts, precise):
    if n_experts:
        router_ref, x_out_ref, h2_ref, gate_ref = rest
    else:
        x_out_ref, h2_ref = rest
    ones = ones_ref[...]
    y = y_ref[...]
    inv_hs = 1.0 / hs
    mu = _dot_exact_rhs(y, ones) * inv_hs
    yc = y - mu
    var = _dot_exact_rhs(yc * yc, ones) * inv_hs
    yn = yc * lax.rsqrt(var + LN_X_EPS) * lng_ref[...] + lnb_ref[...]
    v = v_ref[...]
    bonus = _dot_exact_rhs(r_ref[...] * k_ref[...] * rk_ref[...], ones) * v
    out_a = _mm((yn + bonus) * g_ref[...], wor_ref[...], precise)

    o = o_ref[...]
    parts = []
    for h in range(o.shape[1] // v_dim):
        oh = o[:, h * v_dim:(h + 1) * v_dim]
        parts.append(oh * lax.rsqrt(jnp.mean(oh * oh, axis=-1, keepdims=True) + NORM_EPS))
    on = jnp.concatenate(parts, axis=1) * sub_ref[...] * (1.0 - lam_init)
    out_b = _mm(on, wod_ref[...], precise)

    merged = _sigmoid(ga_ref[...]) * out_a + _sigmoid(gb_ref[...]) * out_b
    x = x_ref[...] + g1_ref[0] * _mm(merged, wout_ref[...], precise)
    x_out_ref[...] = x
    xn = x * lax.rsqrt(jnp.mean(x * x, axis=-1, keepdims=True) + NORM_EPS)
    h2 = (xn * n2_ref[...]) * (1.0 + sc2_ref[0]) + sh2_ref[0]
    h2_ref[...] = h2.astype(h2_ref.dtype)
    if n_experts:
        logits = _dot3(h2, router_ref[...])
        lane = _iota(logits.shape, 1).astype(F32)
        big = float(LANES)
        lg = jnp.where(lane < n_experts, logits, -jnp.inf)
        m1 = jnp.max(lg, axis=-1, keepdims=True)
        i1 = jnp.min(jnp.where(lg == m1, lane, big), axis=-1, keepdims=True)
        lg2 = jnp.where(lane == i1, -jnp.inf, lg)
        m2 = jnp.max(lg2, axis=-1, keepdims=True)
        i2 = jnp.min(jnp.where(lg2 == m2, lane, big), axis=-1, keepdims=True)
        e2 = jnp.exp(m2 - m1)
        den = 1.0 + e2
        gate_ref[...] = jnp.where(lane == i1, 1.0 / den, 0.0) + jnp.where(lane == i2, e2 / den, 0.0)


def _post_call(y, r, k, v, g, o, proj, x, rows, lw, ones_head, gate_cols, lam_init, hs, v_dim):
    n, d = x.shape
    width = y.shape[1]
    tm = _tile(n, 256)
    router = lw.get("router_pad")
    n_experts = lw["n_experts"] if router is not None else 0
    tok = lambda wd: pl.BlockSpec((tm, wd), lambda i: (i, 0))
    row_spec = lambda wd: pl.BlockSpec((1, wd), lambda i: (0, 0))
    full = lambda a: pl.BlockSpec(a.shape, lambda i: (0,) * a.ndim)
    in_specs = [tok(width)] * 6 + [
        pl.BlockSpec((tm, d), lambda i: (i, gate_cols[0])), pl.BlockSpec((tm, d), lambda i: (i, gate_cols[1])),
        tok(d), rows.mod_spec(tm, d, 2, 1), rows.mod_spec(tm, d, 3, 1), rows.mod_spec(tm, d, 4, 1),
        row_spec(width), row_spec(width), row_spec(width), row_spec(width), row_spec(d), full(ones_head),
        full(lw["w_o_rwkv"]), full(lw["w_o_diff"]), full(lw["w_out"]),
    ]
    args = [y, r, k, v, g, o, proj, proj, x, rows.mod3, rows.mod3, rows.mod3,
            lw["ln_x_g"], lw["ln_x_b"], lw["r_k"], lw["subln_g"], lw["norm2_g"], ones_head,
            lw["w_o_rwkv"], lw["w_o_diff"], lw["w_out"]]
    out_specs = [tok(d), tok(d)]
    precise = lw["w_out"].dtype == F32
    out_shape = [jax.ShapeDtypeStruct((n, d), F32), jax.ShapeDtypeStruct((n, d), F32 if precise else BF16)]
    if n_experts:
        in_specs.append(full(router))
        args.append(router)
        out_specs.append(tok(LANES))
        out_shape.append(jax.ShapeDtypeStruct((n, LANES), F32))
    kern = functools.partial(_post_kernel, hs=hs, v_dim=v_dim, lam_init=lam_init, n_experts=n_experts,
                             precise=precise)
    return pl.pallas_call(
        kern,
        grid=(n // tm,),
        in_specs=in_specs,
        out_specs=out_specs,
        out_shape=out_shape,
        compiler_params=_params("parallel"),
        name="merge_post",
    )(*args)


def _ffn_kernel(*refs, moe, precise):
    if moe:
        h_ref, x_ref, g2_ref, gate_ref, w1_ref, w3_ref, w2_ref, o_ref, acc_scr = refs
    else:
        h_ref, x_ref, g2_ref, w1_ref, w3_ref, w2_ref, o_ref, acc_scr = refs
    e = pl.program_id(1)
    f = pl.program_id(2)

    @pl.when((e == 0) & (f == 0))
    def _():
        acc_scr[...] = jnp.zeros(acc_scr.shape, F32)

    h = h_ref[...]
    a = _mm(h, w1_ref[0], precise)
    b = _mm(h, w3_ref[0], precise)
    act = a * _sigmoid(a) * b
    if moe:
        gate = gate_ref[...]
        lane = _iota(gate.shape, 1)
        act = act * jnp.sum(jnp.where(lane == e, gate, 0.0), axis=-1, keepdims=True)
    acc_scr[...] += _mm(act, w2_ref[0], precise)

    @pl.when((e == pl.num_programs(1) - 1) & (f == pl.num_programs(2) - 1))
    def _():
        o_ref[...] = x_ref[...] + g2_ref[0] * acc_scr[...]


def _ffn_call(h2, x, rows, gate, w1, w3, w2, tf):
    n, d = x.shape
    n_exp, _, ff = w1.shape
    tm = _tile(n, 512 if w1.dtype == BF16 else 256)
    moe = gate is not None
    tok = lambda wd: pl.BlockSpec((tm, wd), lambda i, e, f: (i, 0))
    in_specs = [tok(d), tok(d), rows.mod_spec(tm, d, 5, 3)]
    args = [h2, x, rows.mod3]
    if moe:
        in_specs.append(tok(LANES))
        args.append(gate)
    in_specs += [pl.BlockSpec((1, d, tf), lambda i, e, f: (e, 0, f)),
                 pl.BlockSpec((1, d, tf), lambda i, e, f: (e, 0, f)),
                 pl.BlockSpec((1, tf, d), lambda i, e, f: (e, f, 0))]
    args += [w1, w3, w2]
    return pl.pallas_call(
        functools.partial(_ffn_kernel, moe=moe, precise=w1.dtype == F32),
        grid=(n // tm, n_exp, ff // tf),
        in_specs=in_specs,
        out_specs=tok(d),
        out_shape=jax.ShapeDtypeStruct((n, d), F32),
        scratch_shapes=[pltpu.VMEM((tm, d), F32)],
        compiler_params=_params("parallel", "arbitrary", "arbitrary"),
        name="moe_ffn" if moe else "dense_ffn",
    )(*args)


def _ff_tile(ff):
    for t in (1408, 896, 512, 256, 128):
        if ff % t == 0 and ff // t >= 2:
            return t
    return ff


MOE_ROW_BLOCK = 128


def _moe_routed_kernel(nblk_ref, h_ref, x_ref, g2_ref, gate_ref, tri_ref, w1_ref, w3_ref, w2_ref, o_ref,
                       acc_scr, slotc_scr, slotr_scr, hsel_scr, ysel_scr, gsel_scr, *, rb, n_exp):
    i = pl.program_id(0)
    e = pl.program_id(1)
    f = pl.program_id(2)
    last_f = pl.num_programs(2) - 1
    tm = h_ref.shape[0]
    nb = nblk_ref[i * n_exp + e]
    dot = functools.partial(jnp.dot, preferred_element_type=F32)

    @pl.when((e == 0) & (f == 0))
    def _():
        acc_scr[...] = jnp.zeros(acc_scr.shape, F32)
        routed = gate_ref[...] > 0.0
        ind = jnp.where(routed, 1.0, 0.0).astype(BF16)
        tri = tri_ref[...]
        rank = dot(tri, ind)
        slotc_scr[...] = jnp.where(routed, rank, -1.0)
        eye = (_iota((LANES, LANES), 0) == _iota((LANES, LANES), 1)).astype(BF16)
        ind_t = _dot_nt(eye, ind)
        rank_t = _dot_nt(ind_t.astype(BF16), tri)
        slotr_scr[...] = jnp.where(ind_t > 0.5, rank_t, -1.0)

    @pl.when(f == 0)
    def _():
        slot_row = slotr_scr[pl.ds(e, 1), :]

        def select(b, carry):
            r0 = pl.multiple_of(b * rb, rb)
            want = (r0 + _iota((rb, 1), 0)).astype(F32)
            onehot = jnp.where(slot_row == want, 1.0, 0.0).astype(BF16)
            hsel_scr[pl.ds(r0, rb), :] = dot(onehot, h_ref[...]).astype(BF16)
            g = _dot_exact_lhs(onehot, gate_ref[...])
            ge = jnp.sum(jnp.where(_iota(g.shape, 1) == e, g, 0.0), axis=-1, keepdims=True)
            gsel_scr[pl.ds(r0, rb), :] = jnp.broadcast_to(ge, (rb, LANES))
            ysel_scr[pl.ds(r0, rb), :] = jnp.zeros((rb, ysel_scr.shape[1]), F32)
            return carry

        lax.fori_loop(0, nb, select, 0)

    def expert(b, carry):
        r0 = pl.multiple_of(b * rb, rb)
        hs = hsel_scr[pl.ds(r0, rb), :]
        a = dot(hs, w1_ref[0])
        act = a * _sigmoid(a) * dot(hs, w3_ref[0]) * gsel_scr[pl.ds(r0, rb), 0:1]
        ysel_scr[pl.ds(r0, rb), :] += dot(act.astype(BF16), w2_ref[0])
        return carry

    lax.fori_loop(0, nb, expert, 0)

    @pl.when(f == last_f)
    def _():
        slotc = slotc_scr[...]
        slot_col = jnp.sum(jnp.where(_iota(slotc.shape, 1) == e, slotc, 0.0), axis=-1, keepdims=True)

        def scatter(b, carry):
            r0 = pl.multiple_of(b * rb, rb)
            want = (r0 + _iota((1, rb), 1)).astype(F32)
            onehot_t = jnp.where(slot_col == want, 1.0, 0.0).astype(BF16)
            yh, yl = _split2(ysel_scr[pl.ds(r0, rb), :])
            acc_scr[...] += dot(onehot_t, yh) + dot(onehot_t, yl)
            return carry

        lax.fori_loop(0, nb, scatter, 0)

    @pl.when((e == n_exp - 1) & (f == last_f))
    def _():
        o_ref[...] = x_ref[...] + g2_ref[0] * acc_scr[...]


def _moe_routed_call(h2, x, rows, gate, w1, w3, w2, tf):
    n, d = x.shape
    n_exp, _, ff = w1.shape
    rb = MOE_ROW_BLOCK
    tm = _tile(n, 1024)
    n_tiles = n // tm
    counts = jnp.sum((gate[:, :n_exp] > 0.0).reshape(n_tiles, tm, n_exp).astype(jnp.int32), axis=1)
    nblk = ((counts + rb - 1) // rb).reshape(-1)
    t = jnp.arange(tm)
    tri = (t[None, :] < t[:, None]).astype(BF16)
    tok = lambda wd: pl.BlockSpec((tm, wd), lambda i, e, f, nb: (i, 0))
    mod = rows.mod_spec(tm, d, 5, 3)
    grid_spec = pltpu.PrefetchScalarGridSpec(
        num_scalar_prefetch=1,
        grid=(n_tiles, n_exp, ff // tf),
        in_specs=[tok(d), tok(d), pl.BlockSpec(mod.block_shape, lambda i, e, f, nb: mod.index_map(i)), tok(LANES),
                  pl.BlockSpec((tm, tm), lambda i, e, f, nb: (0, 0)),
                  pl.BlockSpec((1, d, tf), lambda i, e, f, nb: (e, 0, f)),
                  pl.BlockSpec((1, d, tf), lambda i, e, f, nb: (e, 0, f)),
                  pl.BlockSpec((1, tf, d), lambda i, e, f, nb: (e, f, 0))],
        out_specs=tok(d),
        scratch_shapes=[pltpu.VMEM((tm, d), F32), pltpu.VMEM((tm, LANES), F32), pltpu.VMEM((LANES, tm), F32),
                        pltpu.VMEM((tm, d), BF16), pltpu.VMEM((tm, d), F32), pltpu.VMEM((tm, LANES), F32)],
    )
    return pl.pallas_call(
        functools.partial(_moe_routed_kernel, rb=rb, n_exp=n_exp),
        grid_spec=grid_spec,
        out_shape=jax.ShapeDtypeStruct((n, d), F32),
        compiler_params=_params("parallel", "arbitrary", "arbitrary"),
        name="moe_routed",
    )(nblk, h2, x, rows.mod3, gate, tri, w1, w3, w2)


def _block_ones(width, seg):
    i = jnp.arange(width) // seg
    return (i[:, None] == i[None, :]).astype(BF16)


def _rope_tables(pos, qk_dim, n_seg):
    half = qk_dim // 2
    inv_freq = ROPE_THETA ** (-jnp.arange(half, dtype=F32) / half)
    ang = pos.astype(F32)[:, None] * inv_freq[None, :]
    cos, sin = jnp.cos(ang), jnp.sin(ang)
    cos_t = jnp.tile(jnp.concatenate([cos, cos], axis=1), (1, n_seg))
    sin_t = jnp.tile(jnp.concatenate([-sin, sin], axis=1), (1, n_seg))
    return cos_t, sin_t


def kernel(x_prompt, x_sample, c_prompt, c_sample, cache_k, cache_v, state_wkv, state_shift, page_table,
           norm1_g, norm2_g, w_ada, b_ada, w_in, mu_shift, w0, w_up, a0, a_up, g_up, k_k, k_a, r_k,
           ln_x_g, ln_x_b, w_o_rwkv, q_norm_g, k_norm_g, lam, subln_g, w_o_diff, w_out,
           ffn_w1, ffn_w3, ffn_w2, router, moe_w1, moe_w3, moe_w2):
    n_batch, seq, d = x_prompt.shape
    n_dec, dec_seq, _ = x_sample.shape
    depth = w_in.shape[0]
    heads_r, hs = r_k.shape[1], r_k.shape[2]
    width_r = heads_r * hs
    proj_r = mu_shift.shape[-1]
    n_phys, page, heads_d, _, qk_dim = cache_k.shape[1:]
    v_dim = cache_v.shape[-1]
    qk_w = heads_d * 2 * qk_dim
    v_w = heads_d * v_dim
    w_lora, a_lora = w_up.shape[1], a_up.shape[1]
    assert v_dim == 2 * qk_dim == LANES and qk_w == v_w and w_lora + a_lora == LANES
    past_len = page_table.shape[1] * page
    n_p, n_s = n_batch * seq, n_dec * dec_seq

    o_q = proj_r
    o_k, o_v, o_g = o_q + qk_w, o_q + 2 * qk_w, o_q + 2 * qk_w + v_w
    w_in_f = jnp.concatenate([w_in[:, :, o_g:], w_in[:, :, o_q:o_g], w_in[:, :, :o_q]], axis=-1)
    w_in_b = w_in_f.astype(BF16)
    n_gate = w_in.shape[-1] - o_g
    assert n_gate == 2 * d and (n_gate + 3 * qk_w) % proj_r == 0 and n_gate % qk_w == 0
    gate_cols = (0, 1)
    qkv_cols = tuple(n_gate // qk_w + j for j in range(3))
    rwkv_col = (n_gate + 3 * qk_w) // proj_r
    tn = proj_r

    c_all = jnp.concatenate([c_prompt, c_sample], axis=0)
    c_all = jnp.pad(c_all, ((0, (-c_all.shape[0]) % 8), (0, 0)))
    mod = _ada_call(c_all, w_ada, b_ada)

    ones_head = _block_ones(width_r, hs)
    ones_qk = _block_ones(qk_w, qk_dim)
    cos_p, sin_p = _rope_tables(jnp.arange(seq), qk_dim, qk_w // qk_dim)
    cos_s, sin_s = _rope_tables(past_len + jnp.arange(dec_seq), qk_dim, qk_w // qk_dim)
    cos_s, sin_s = jnp.tile(cos_s, (n_dec, 1)), jnp.tile(sin_s, (n_dec, 1))
    cache_kt = jnp.transpose(cache_k, (0, 1, 3, 4, 5, 2))
    cache_v4 = cache_v.reshape(depth, n_phys, page * heads_d, v_dim)

    zpad = lambda m, before, after: jnp.pad(m, ((before, after), (0, 0)))
    bf = lambda a: a.astype(BF16)
    xp = x_prompt.reshape(n_p, d)
    xs = x_sample.reshape(n_s, d)
    shift0 = jnp.zeros((n_batch, 1, proj_r), F32)
    wkv_zero = jnp.zeros((n_batch, heads_r, hs, hs), F32)
    wkv_chunk = _tile(seq, 64)
    dec_chunk = max(16, dec_seq)
    outs = {key: [] for key in ("kp", "vp", "ks", "vs", "wp", "ws", "sp", "ss")}

    for l in range(depth):
        lam_init = 0.8 - 0.6 * math.exp(-0.3 * l)
        lw = {
            "mu_shift": mu_shift[l][None], "w0": w0[l][None], "a0": a0[l][None],
            "w_up_pad": zpad(w_up[l], 0, a_lora), "a_up_pad": zpad(a_up[l], w_lora, 0), "g_up": g_up[l],
            "k_k": k_k[l][None], "k_a": k_a[l][None], "r_k": r_k[l].reshape(1, width_r),
            "ln_x_g": ln_x_g[l][None], "ln_x_b": ln_x_b[l][None],
            "subln_g": jnp.tile(subln_g[l], heads_d)[None], "norm2_g": norm2_g[l][None],
        }
        qg = jnp.tile(q_norm_g[l].reshape(-1), heads_d)[None]
        kg = jnp.tile(k_norm_g[l].reshape(-1), heads_d)[None]
        if l % 2 == 0:
            i = l // 2
            ffn_f = (ffn_w1[i][None], ffn_w3[i][None], ffn_w2[i][None])
        else:
            i = l // 2
            n_experts = router.shape[-1]
            lw["router_pad"] = jnp.pad(router[i], ((0, 0), (0, LANES - n_experts)))
            lw["n_experts"] = n_experts
            ffn_f = (moe_w1[i], moe_w3[i], moe_w2[i])
        tf = _ff_tile(ffn_f[0].shape[-1])
        norm1 = norm1_g[l][None]
        out_w = {"w_o_rwkv": w_o_rwkv[l], "w_o_diff": w_o_diff[l], "w_out": w_out[l]}
        lw_p = dict(lw, **{key: bf(val) for key, val in out_w.items()})
        lw_s = dict(lw, **out_w)
        ffn_b = tuple(bf(w) for w in ffn_f)

        def run_group(x, rows, init_rows, state0, is_prompt):
            lw_g, w_in_g, ffn = (lw_p, w_in_b[l], ffn_b) if is_prompt else (lw_s, w_in_f[l], ffn_f)
            proj = _proj_call(x, rows, norm1, w_in_g, tn)
            r, ld, k2, v_r, av, bv, g = _rwkv_prep_call(proj, rows, init_rows, lw_g, rwkv_col, ones_head,
                                                        not is_prompt)
            if is_prompt:
                y, state = _wkv_call(r, ld, k2, v_r, av, bv, state0, rows.seq_len, wkv_chunk, False,
                                     n_sub=WKV_CHUNKS_PER_STEP if seq % (wkv_chunk * WKV_CHUNKS_PER_STEP) == 0 else 1)
                k_rows, v_rows, q_bf, k_bf, v_aug = _attn_prep_call(
                    proj, rows, cos_p, sin_p, qg, kg, ones_qk, qkv_cols, BF16, qk_dim)
                o = _flash_call(q_bf, k_bf, v_aug, lam[l], n_batch, seq, heads_d, qk_dim, lam_init)
            else:
                padt = lambda a: jnp.pad(a.reshape(n_dec, dec_seq, width_r),
                                         ((0, 0), (0, dec_chunk - dec_seq), (0, 0))).reshape(-1, width_r)
                y, state = _wkv_call(padt(r), padt(ld), padt(k2), padt(v_r), padt(av), padt(bv), state0,
                                     dec_chunk, dec_chunk, True)
                y = y.reshape(n_dec, dec_chunk, width_r)[:, :dec_seq].reshape(-1, width_r)
                k_rows, v_rows, q_f, _, _ = _attn_prep_call(
                    proj, rows, cos_s, sin_s, qg, kg, ones_qk, qkv_cols, F32, qk_dim)
                o = _paged_call(page_table, lam[l], q_f, k_rows, v_rows, cache_kt, cache_v4, l, heads_d,
                                qk_dim, lam_init, dec_seq)
            post = _post_call(y, r, k2, v_r, g, o, proj, x, rows, lw_g, ones_head, gate_cols, lam_init, hs, v_dim)
            x_new, h2 = post[0], post[1]
            gate = post[2] if len(post) > 2 else None
            ffn_call = _moe_routed_call if (is_prompt and gate is not None) else _ffn_call
            x_out = ffn_call(h2, x_new, rows, gate, *ffn, tf)
            shift = proj.reshape(-1, rows.seq_len, proj.shape[-1])[:, -1, rwkv_col * proj_r:]
            return x_out, k_rows, v_rows, state, shift

        rows_p = _Rows(n_p, seq, mod[l, :n_batch][:, None, :])
        rows_s = _Rows(n_s, dec_seq, jnp.repeat(mod[l, n_batch:n_batch + n_dec], dec_seq, axis=0)[None])
        xp, kr, vr, st, sh = run_group(xp, rows_p, shift0, wkv_zero, True)
        outs["kp"].append(kr.reshape(n_batch, seq, heads_d, 2, qk_dim))
        outs["vp"].append(vr.reshape(n_batch, seq, heads_d, v_dim))
        outs["wp"].append(st)
        outs["sp"].append(sh)
        init_s = jnp.repeat(state_shift[l], dec_seq, axis=0)
        xs, kr, vr, st, sh = run_group(xs, rows_s, init_s, state_wkv[l], False)
        outs["ks"].append(kr.reshape(n_dec, dec_seq, heads_d, 2, qk_dim))
        outs["vs"].append(vr.reshape(n_dec, dec_seq, heads_d, v_dim))
        outs["ws"].append(st)
        outs["ss"].append(sh)

    st = lambda key: jnp.stack(outs[key])
    return (xp.reshape(n_batch, seq, d), xs.reshape(n_dec, dec_seq, d), st("kp"), st("vp"), st("ks"), st("vs"),
            st("wp"), st("ws"), st("sp"), st("ss"))
```

```python
import functools
import math

import jax
import jax.numpy as jnp
from jax import lax
from jax.experimental import pallas as pl
from jax.experimental.pallas import tpu as pltpu

F32 = jnp.float32
BF16 = jnp.bfloat16

NORM_EPS = 1e-6
LN_X_EPS = 64e-5
ROPE_THETA = 10000.0
TOP_K = 2
KK_NORM_FLOOR = 1e-12
LANES = 128
VMEM_LIMIT = 56 * 1024 * 1024
FLASH_ROW_PARTS = 4
WKV_CHUNKS_PER_STEP = 4


def _params(*sem):
    return pltpu.CompilerParams(dimension_semantics=sem, vmem_limit_bytes=VMEM_LIMIT)


def _tile(n, pref):
    if n <= pref:
        return n
    t = pref
    while n % t:
        t //= 2
    return t


def _bdot(a, b):
    return jnp.dot(a.astype(BF16), b.astype(BF16), preferred_element_type=F32)


def _mm(a, b, precise):
    return _dot3(a.astype(F32), b.astype(F32)) if precise else _bdot(a, b)


def _dot_nt(a, b):
    return lax.dot_general(a, b, (((1,), (1,)), ((), ())), preferred_element_type=F32)


def _split2(x):
    hi = x.astype(BF16)
    lo = (x - hi.astype(F32)).astype(BF16)
    return hi, lo


def _split3(x):
    hi = x.astype(BF16)
    r1 = x - hi.astype(F32)
    mid = r1.astype(BF16)
    lo = (r1 - mid.astype(F32)).astype(BF16)
    return hi, mid, lo


def _dot3(a, b):
    ah, al = _split2(a)
    bh, bl = _split2(b)
    d = functools.partial(jnp.dot, preferred_element_type=F32)
    return d(ah, bh) + d(ah, bl) + d(al, bh)


def _dot3_nt(a, b):
    ah, al = _split2(a)
    bh, bl = _split2(b)
    return _dot_nt(ah, bh) + _dot_nt(ah, bl) + _dot_nt(al, bh)


def _dot_exact_rhs(x, m_bf):
    hi, mid, lo = _split3(x)
    d = functools.partial(jnp.dot, preferred_element_type=F32)
    return d(hi, m_bf) + d(mid, m_bf) + d(lo, m_bf)


def _dot_exact_lhs(m_bf, x):
    hi, mid, lo = _split3(x)
    d = functools.partial(jnp.dot, preferred_element_type=F32)
    return d(m_bf, hi) + d(m_bf, mid) + d(m_bf, lo)


def _sigmoid(x):
    return 1.0 / (1.0 + jnp.exp(-x))


def _softplus(x):
    return jnp.maximum(x, 0.0) + jnp.log(1.0 + jnp.exp(-jnp.abs(x)))


def _iota(shape, dim):
    return lax.broadcasted_iota(jnp.int32, shape, dim)


def _ada_kernel(c_ref, w_ref, b_ref, o_ref):
    o_ref[0] = _dot3(c_ref[...], w_ref[0]) + b_ref[0]


def _ada_call(c_all, w_ada, b_ada):
    n_layers, d, d6 = w_ada.shape
    m = c_all.shape[0]
    tn = _tile(d6, 1536)
    return pl.pallas_call(
        _ada_kernel,
        grid=(n_layers, d6 // tn),
        in_specs=[
            pl.BlockSpec((m, d), lambda l, j: (0, 0)),
            pl.BlockSpec((1, d, tn), lambda l, j: (l, 0, j)),
            pl.BlockSpec((1, 1, tn), lambda l, j: (l, 0, j)),
        ],
        out_specs=pl.BlockSpec((1, m, tn), lambda l, j: (l, 0, j)),
        out_shape=jax.ShapeDtypeStruct((n_layers, m, d6), F32),
        compiler_params=_params("parallel", "parallel"),
        name="adaln_mod",
    )(c_all, w_ada, b_ada.reshape(n_layers, 1, d6))


class _Rows:
    def __init__(self, n, seq_len, mod3):
        self.n, self.seq_len, self.mod3 = n, seq_len, mod3
        self.per_row = mod3.shape[1] != 1

    def mod_spec(self, tm, d, col, n_grid):
        r = self.mod3.shape[1]
        if self.per_row:
            assert r == tm == self.n
            idx = lambda i, *_: (0, 0, col)
        else:
            assert self.seq_len % tm == 0
            per = self.seq_len // tm
            idx = lambda i, *_: (i // per, 0, col)
        return pl.BlockSpec((1, r, d), idx)


def _proj_kernel(x_ref, sh_ref, sc_ref, g_ref, w_ref, o_ref, h_scr, *, precise):
    @pl.when(pl.program_id(1) == 0)
    def _():
        x = x_ref[...]
        y = x * lax.rsqrt(jnp.mean(x * x, axis=-1, keepdims=True) + NORM_EPS)
        h = (y * g_ref[...]) * (1.0 + sc_ref[0]) + sh_ref[0]
        h_scr[...] = h.astype(h_scr.dtype)

    o_ref[...] = _mm(h_scr[...], w_ref[...], precise)


def _proj_call(x, rows, norm_g, w, tn):
    n, d = x.shape
    pw = w.shape[1]
    tm = _tile(n, 512)
    precise = w.dtype == F32
    return pl.pallas_call(
        functools.partial(_proj_kernel, precise=precise),
        grid=(n // tm, pw // tn),
        in_specs=[
            pl.BlockSpec((tm, d), lambda i, j: (i, 0)),
            rows.mod_spec(tm, d, 0, 2),
            rows.mod_spec(tm, d, 1, 2),
            pl.BlockSpec((1, d), lambda i, j: (0, 0)),
            pl.BlockSpec((d, tn), lambda i, j: (0, j)),
        ],
        out_specs=pl.BlockSpec((tm, tn), lambda i, j: (i, j)),
        out_shape=jax.ShapeDtypeStruct((n, pw), F32),
        scratch_shapes=[pltpu.VMEM((tm, d), w.dtype)],
        compiler_params=_params("parallel", "arbitrary"),
        name="proj_in",
    )(x, rows.mod3, rows.mod3, norm_g, w)


def _rwkv_prep_kernel(cols_ref, prev8_ref, init_ref, mu_ref, w0_ref, wup_ref, a0_ref, aup_ref,
                      gup_ref, kk_ref, ka_ref, ones_ref,
                      r_ref, ld_ref, k_ref, v_ref, av_ref, bv_ref, g_ref, *, seq_len, tm, width, precise):
    cols = cols_ref[...]
    rolled = pltpu.roll(cols, 1, 0)
    row = _iota((tm, 1), 0)
    if seq_len >= tm:
        first_tile = (pl.program_id(0) % (seq_len // tm)) == 0
        boundary = jnp.where(first_tile, init_ref[0], prev8_ref[7:8, :])
        prev = jnp.where(row == 0, boundary, rolled)
    else:
        prev = jnp.where(row % seq_len == 0, init_ref[...], rolled)
    xs = cols + (prev - cols) * mu_ref[...]
    w = width
    r = xs[:, 0:w]
    k = xs[:, w:2 * w]
    v = xs[:, 2 * w:3 * w]
    lora_wa = xs[:, 3 * w:3 * w + LANES]
    gd = xs[:, 3 * w + LANES:]
    w_log = -_softplus(-(w0_ref[...] + _mm(jnp.tanh(lora_wa), wup_ref[...], precise))) - 0.5
    a = _sigmoid(a0_ref[...] + _mm(lora_wa, aup_ref[...], precise))
    g = _mm(_sigmoid(gd), gup_ref[...], precise)
    kk = k * kk_ref[...]
    ss = _dot_exact_rhs(kk * kk, ones_ref[...])
    kk = kk / jnp.maximum(jnp.sqrt(ss), KK_NORM_FLOOR)
    r_ref[...] = r
    ld_ref[...] = -jnp.exp(w_log)
    k_ref[...] = k * (1.0 + (a - 1.0) * ka_ref[...])
    v_ref[...] = v
    av_ref[...] = -kk
    bv_ref[...] = kk * a
    g_ref[...] = g


def _rwkv_prep_call(proj, rows, init_rows, lw, col_block, ones_head, precise):
    n = proj.shape[0]
    width = lw["w0"].shape[-1]
    pw = lw["mu_shift"].shape[-1]
    tm = _tile(n, 256) if rows.seq_len >= 256 else n
    assert pw == 3 * width + LANES + lw["g_up"].shape[0]
    if rows.seq_len >= tm:
        init_spec = pl.BlockSpec((1, 1, pw), lambda i: ((i * tm) // rows.seq_len, 0, 0))
    else:
        init_spec = pl.BlockSpec((tm, pw), lambda i: (0, 0))
    row_spec = lambda wd: pl.BlockSpec((1, wd), lambda i: (0, 0))
    full = lambda a: pl.BlockSpec(a.shape, lambda i: (0,) * a.ndim)
    out = jax.ShapeDtypeStruct((n, width), F32)
    kern = functools.partial(_rwkv_prep_kernel, seq_len=rows.seq_len, tm=tm, width=width, precise=precise)
    return pl.pallas_call(
        kern,
        grid=(n // tm,),
        in_specs=[
            pl.BlockSpec((tm, pw), lambda i: (i, col_block)),
            pl.BlockSpec((8, pw), lambda i: (jnp.maximum(i * (tm // 8) - 1, 0), col_block)),
            init_spec,
            row_spec(pw), row_spec(width), full(lw["w_up_pad"]), row_spec(width), full(lw["a_up_pad"]),
            full(lw["g_up"]), row_spec(width), row_spec(width), full(ones_head),
        ],
        out_specs=[pl.BlockSpec((tm, width), lambda i: (i, 0))] * 7,
        out_shape=[out] * 7,
        compiler_params=_params("parallel"),
        name="rwkv_prep",
    )(proj, proj, init_rows, lw["mu_shift"], lw["w0"], lw["w_up_pad"], lw["a0"], lw["a_up_pad"],
      lw["g_up"], lw["k_k"], lw["k_a"], ones_head)


def _wkv_kernel(r_ref, ld_ref, k_ref, v_ref, a_ref, b_ref, s0_ref, y_ref, s_ref, *, chunk, heads, hs, precise):
    c = chunk
    if precise:
        mm, mm_nt = _dot3, _dot3_nt
    else:
        mm = _bdot
        mm_nt = lambda a, b: _dot_nt(a.astype(BF16), b.astype(BF16))

    @pl.when(pl.program_id(1) == 0)
    def _():
        s_ref[...] = s0_ref[...]

    rr = _iota((c, c), 0)
    cc = _iota((c, c), 1)
    strict = rr > cc
    incl = rr >= cc
    tri = incl.astype(BF16)
    eye_c = (rr == cc).astype(F32)
    eye_v = (_iota((hs, hs), 0) == _iota((hs, hs), 1)).astype(BF16)

    n_double = int(math.log2(c)) - 1
    hr = range(heads)
    sls = [slice(h * hs, (h + 1) * hs) for h in hr]
    n_sub = r_ref.shape[0] // c
    items = [(ci, h) for ci in range(n_sub) for h in hr]

    pre = []
    for ci in range(n_sub):
        rows = slice(ci * c, (ci + 1) * c)
        ld = ld_ref[rows, :]
        cum = _dot_exact_lhs(tri, ld)
        tot = cum[c - 1:c, :]
        e_neg = jnp.exp(-cum)
        e_dec = jnp.exp(tot - cum)
        b_all, k_all = b_ref[rows, :], k_ref[rows, :]
        pre.append(dict(a_t=a_ref[rows, :] * jnp.exp(cum - ld), r_t=r_ref[rows, :] * jnp.exp(cum),
                        b_t=b_all * e_neg, k_t=k_all * e_neg, b_d=b_all * e_dec, k_d=k_all * e_dec,
                        e_tot=jnp.exp(tot), v=v_ref[rows, :]))
    get = lambda name, ci, h: pre[ci][name][:, sls[h]]

    pm = [mm_nt(jnp.concatenate([get("a_t", ci, h), get("r_t", ci, h)], axis=0),
                jnp.concatenate([get("b_t", ci, h), get("k_t", ci, h)], axis=0)) for ci, h in items]
    l_ab = [jnp.where(strict, p[:c, :c], 0.0) for p in pm]
    l_ak = [jnp.where(strict, p[:c, c:], 0.0) for p in pm]
    m_rb = [jnp.where(incl, p[c:, :c], 0.0) for p in pm]
    m_rk = [jnp.where(incl, p[c:, c:], 0.0) for p in pm]
    t_inv = [eye_c + l for l in l_ab]
    l_pow = l_ab
    for _ in range(n_double):
        l_pow = [mm(l, l) for l in l_pow]
        t_inv = [t + mm(t, l) for t, l in zip(t_inv, l_pow)]
    vh = [get("v", ci, h) for ci, h in items]
    z = [mm(l_ak[j], vh[j]) for j in range(len(items))]
    au = [mm(t_inv[j], jnp.concatenate([get("a_t", ci, h), z[j]], axis=1)) for j, (ci, h) in enumerate(items)]
    state = [s_ref[0, h] for h in hr]
    for ci in range(n_sub):
        rows = slice(ci * c, (ci + 1) * c)
        for h in hr:
            j = ci * heads + h
            s0 = state[h]
            u = mm_nt(au[j][:, :hs], s0) + au[j][:, hs:]
            y = mm_nt(get("r_t", ci, h), s0) + mm(m_rb[j], u) + mm(m_rk[j], vh[j])
            uv = jnp.concatenate([u, vh[j]], axis=0)
            if precise:
                uh, um, ul = _split3(uv)
                uv_t = _dot_nt(eye_v, uh) + _dot_nt(eye_v, um) + _dot_nt(eye_v, ul)
            else:
                uv_t = _dot_nt(eye_v, uv.astype(BF16))
            bkd = jnp.concatenate([get("b_d", ci, h), get("k_d", ci, h)], axis=0)
            state[h] = s0 * pre[ci]["e_tot"][:, sls[h]] + mm(uv_t, bkd)
            y_ref[rows, sls[h]] = y
    for h in hr:
        s_ref[0, h] = state[h]


def _wkv_call(r, ld, k, v, av, bv, state0, seq_len, chunk, precise, n_sub=1):
    n, width = r.shape
    nb, heads, hs, _ = state0.shape
    n_chunks = seq_len // (chunk * n_sub)
    tok = pl.BlockSpec((chunk * n_sub, width), lambda b, c: (b * n_chunks + c, 0))
    st = pl.BlockSpec((1, heads, hs, hs), lambda b, c: (b, 0, 0, 0))
    kern = functools.partial(_wkv_kernel, chunk=chunk, heads=heads, hs=hs, precise=precise)
    return pl.pallas_call(
        kern,
        grid=(nb, n_chunks),
        in_specs=[tok] * 6 + [st],
        out_specs=[tok, st],
        out_shape=[jax.ShapeDtypeStruct((n, width), F32), jax.ShapeDtypeStruct(state0.shape, F32)],
        compiler_params=_params("parallel", "arbitrary"),
        name="wkv_scan",
    )(r, ld, k, v, av, bv, state0)


def _attn_prep_kernel(q_ref, k_ref, v_ref, cos_ref, sin_ref, qg_ref, kg_ref, ones_ref,
                      kr_ref, vr_ref, qo_ref, kb_ref, vb_ref, *, qk_dim, scale):
    ones = ones_ref[...]
    cos = cos_ref[...]
    sin = sin_ref[...]
    lane = _iota(cos.shape, 1)
    first_half = (lane % qk_dim) < (qk_dim // 2)
    width = cos.shape[1]
    half = qk_dim // 2

    def norm_rope(x, g):
        ms = _dot_exact_rhs(x * x, ones) * (1.0 / qk_dim)
        y = x * lax.rsqrt(ms + NORM_EPS) * g
        swapped = jnp.where(first_half, pltpu.roll(y, width - half, 1), pltpu.roll(y, half, 1))
        return y * cos + swapped * sin

    q = norm_rope(q_ref[...], qg_ref[...])
    k = norm_rope(k_ref[...], kg_ref[...])
    v = v_ref[...]
    kr_ref[...] = k
    vr_ref[...] = v
    qo_ref[...] = (q * scale).astype(qo_ref.dtype)
    kb_ref[...] = k.astype(BF16)
    vb = v.astype(BF16)
    hw = 2 * qk_dim
    one = jnp.ones((vb.shape[0], hw), BF16)
    vb_ref[...] = jnp.concatenate(
        [blk for h in range(width // hw) for blk in (vb[:, h * hw:(h + 1) * hw], one)], axis=1)


def _attn_prep_call(proj, rows, cos, sin, qg, kg, ones_qk, col_blocks, q_dtype, qk_dim):
    n = proj.shape[0]
    width = qg.shape[-1]
    tm = _tile(n, 512) if rows.seq_len >= 512 else n
    per = max(rows.seq_len // tm, 1)
    tab = pl.BlockSpec((tm, width), lambda i: (i % per, 0))
    col = lambda cb: pl.BlockSpec((tm, width), lambda i: (i, cb))
    row_spec = pl.BlockSpec((1, width), lambda i: (0, 0))
    out_spec = pl.BlockSpec((tm, width), lambda i: (i, 0))
    kern = functools.partial(_attn_prep_kernel, qk_dim=qk_dim, scale=qk_dim ** -0.5)
    sds = lambda dt: jax.ShapeDtypeStruct((n, width), dt)
    return pl.pallas_call(
        kern,
        grid=(n // tm,),
        in_specs=[col(col_blocks[0]), col(col_blocks[1]), col(col_blocks[2]), tab, tab, row_spec, row_spec,
                  pl.BlockSpec(ones_qk.shape, lambda i: (0, 0))],
        out_specs=[out_spec] * 4 + [pl.BlockSpec((tm, 2 * width), lambda i: (i, 0))],
        out_shape=[sds(F32), sds(F32), sds(q_dtype), sds(BF16), jax.ShapeDtypeStruct((n, 2 * width), BF16)],
        compiler_params=_params("parallel"),
        name="attn_prep",
    )(proj, proj, proj, cos, sin, qg, kg, ones_qk)


def _lambda(lam_ref, lam_init):
    lp = lam_ref[...]
    s1 = jnp.sum(lp[0:1] * lp[1:2], axis=-1, keepdims=True)
    s2 = jnp.sum(lp[2:3] * lp[3:4], axis=-1, keepdims=True)
    return jnp.exp(s1) - jnp.exp(s2) + lam_init


def _stack_maps(qh, qk_dim):
    lane = _iota(qh.shape, 1)
    zero = jnp.zeros_like(qh)
    return jnp.concatenate([jnp.where(lane < qk_dim, qh, zero), jnp.where(lane >= qk_dim, qh, zero)], axis=0)


def _flash_kernel(lam_ref, q_ref, k_ref, v_ref, o_ref, qs_scr, m_scr, acc_scr, *, tq, tk, qk_dim, lam_init,
                  n_parts):
    i = pl.program_id(2)
    rows = 2 * tq
    hw = 2 * qk_dim
    n_blk = tk // hw
    qs_scr[...] = _stack_maps(q_ref[...], qk_dim)
    m_scr[...] = jnp.full(m_scr.shape, -jnp.inf, F32)
    acc_scr[...] = jnp.zeros(acc_scr.shape, F32)
    n_full = (i * tq) // tk

    part = rows // n_parts

    def step(j, masked):
        start = pl.multiple_of(j * tk, tk)
        kc = k_ref[pl.ds(start, tk), :]
        vc = v_ref[pl.ds(start, tk), :]
        for r0 in range(0, rows, part):
            rs = slice(r0, r0 + part)
            s = _dot_nt(qs_scr[rs, :], kc)
            if masked:
                qpos = i * tq + (r0 % tq) + _iota((part, tk), 0)
                kpos = start + _iota((part, tk), 1)
                s = jnp.where(kpos <= qpos, s, -jnp.inf)
            blocks = [s[:, c * hw:(c + 1) * hw] for c in range(n_blk)]
            blk_max = blocks[0]
            for blk in blocks[1:]:
                blk_max = jnp.maximum(blk_max, blk)
            m_prev = m_scr[rs, :]
            m_new = jnp.maximum(m_prev, jnp.max(blk_max, axis=-1, keepdims=True))
            alpha = jnp.exp(m_prev - m_new)
            p = jnp.concatenate([jnp.exp(blk - m_new).astype(BF16) for blk in blocks], axis=1)
            pv = jnp.dot(p, vc, preferred_element_type=F32)
            acc_scr[rs, :] = jnp.concatenate([alpha, alpha], axis=1) * acc_scr[rs, :] + pv
            m_scr[rs, :] = m_new

    def body(j, carry):
        step(j, False)
        return carry

    lax.fori_loop(0, n_full, body, 0)
    step(n_full, True)
    acc = acc_scr[...]
    o = acc[:, :hw] / acc[:, hw:]
    lam = _lambda(lam_ref, lam_init)
    o_ref[...] = o[:tq] - lam * o[tq:]


def _flash_call(q_bf, k_bf, v_aug, lam, n_batch, seq_len, heads, qk_dim, lam_init):
    n, width = q_bf.shape
    hw = width // heads
    tq = _tile(seq_len, 512)
    tk = _tile(seq_len, 1024)
    nq = seq_len // tq
    kern = functools.partial(_flash_kernel, tq=tq, tk=tk, qk_dim=qk_dim, lam_init=lam_init,
                             n_parts=FLASH_ROW_PARTS)
    qspec = pl.BlockSpec((tq, hw), lambda b, h, i: (b * nq + i, h))
    return pl.pallas_call(
        kern,
        grid=(n_batch, heads, nq),
        in_specs=[pl.BlockSpec(lam.shape, lambda b, h, i: (0, 0)), qspec,
                  pl.BlockSpec((seq_len, hw), lambda b, h, i: (b, h)),
                  pl.BlockSpec((seq_len, 2 * hw), lambda b, h, i: (b, h))],
        out_specs=qspec,
        out_shape=jax.ShapeDtypeStruct((n, width), F32),
        scratch_shapes=[pltpu.VMEM((2 * tq, hw), BF16), pltpu.VMEM((2 * tq, hw), F32),
                        pltpu.VMEM((2 * tq, 2 * hw), F32)],
        compiler_params=_params("parallel", "parallel", "arbitrary"),
        name="prompt_attention",
    )(lam, q_bf, k_bf, v_aug)


def _paged_kernel(pt_ref, lam_ref, q_ref, kn_ref, vn_ref, *refs, pages, heads, qk_dim, lam_init, n_groups):
    del pt_ref
    k_refs = refs[:pages]
    v_refs = refs[pages:2 * pages]
    o_ref = refs[2 * pages]
    m_scr, l_scr, acc_scr = refs[2 * pages + 1:]
    g = pl.program_id(1)
    hw = 2 * qk_dim
    ts = q_ref.shape[0]

    @pl.when(g == 0)
    def _():
        m_scr[...] = jnp.full(m_scr.shape, -jnp.inf, F32)
        l_scr[...] = jnp.zeros(l_scr.shape, F32)
        acc_scr[...] = jnp.zeros(acc_scr.shape, F32)

    q = q_ref[...]

    def update(h, s, pv_fn):
        m_prev = m_scr[h]
        m_new = jnp.maximum(m_prev, jnp.max(s, axis=-1, keepdims=True))
        alpha = jnp.exp(m_prev - m_new)
        p = jnp.exp(s - m_new)
        l_scr[h] = alpha * l_scr[h] + jnp.sum(p, axis=-1, keepdims=True)
        acc_scr[h] = alpha * acc_scr[h] + pv_fn(p)
        m_scr[h] = m_new

    def scores(qs, keys):
        qh, ql = _split2(qs)
        kh, kl = _split2(keys)
        both = _dot_nt(jnp.concatenate([qh, ql], axis=0), kh)
        return both[:2 * ts] + both[2 * ts:] + _dot_nt(qh, kl)

    def weighted(p, vals):
        ph, plo = _split2(p)
        vh, vl = _split2(vals)
        both = jnp.dot(jnp.concatenate([ph, plo], axis=0), vh, preferred_element_type=F32)
        return both[:2 * ts] + both[2 * ts:] + jnp.dot(ph, vl, preferred_element_type=F32)

    def scores_t(qs, keys_t):
        qh, ql = _split2(qs)
        kh, kl = _split2(keys_t)
        both = jnp.dot(jnp.concatenate([qh, ql], axis=0), kh, preferred_element_type=F32)
        return both[:2 * ts] + both[2 * ts:] + jnp.dot(qh, kl, preferred_element_type=F32)

    page = k_refs[0].shape[-1]
    hr = range(heads)
    qs = [_stack_maps(q[:, h * hw:(h + 1) * hw], qk_dim) for h in hr]
    keys_t = [jnp.concatenate([k_refs[p][0, 0, h].reshape(hw, page) for p in range(pages)], axis=1) for h in hr]
    s = [scores_t(qs[h], keys_t[h]) for h in hr]
    m_prev = [m_scr[h] for h in hr]
    m_new = [jnp.maximum(m_prev[h], jnp.max(s[h], axis=-1, keepdims=True)) for h in hr]
    alpha = [jnp.exp(m_prev[h] - m_new[h]) for h in hr]
    prob = [jnp.exp(s[h] - m_new[h]) for h in hr]
    vals = [jnp.concatenate([v_refs[p][0, 0, pl.ds(h, page, stride=heads), :] for p in range(pages)], axis=0)
            for h in hr]
    pv = [weighted(prob[h], vals[h]) for h in hr]
    for h in hr:
        l_scr[h] = alpha[h] * l_scr[h] + jnp.sum(prob[h], axis=-1, keepdims=True)
        acc_scr[h] = alpha[h] * acc_scr[h] + pv[h]
        m_scr[h] = m_new[h]

    @pl.when(g == n_groups - 1)
    def _():
        lam = _lambda(lam_ref, lam_init)
        pad = jnp.zeros((LANES - ts, kn_ref.shape[1]), F32)
        kn = jnp.concatenate([kn_ref[...], pad], axis=0)
        vn = jnp.concatenate([vn_ref[...], pad], axis=0)
        qpos = _iota((2 * ts, LANES), 0) % ts
        kpos = _iota((2 * ts, LANES), 1)
        keep = kpos <= qpos
        for h in range(heads):
            sl = slice(h * hw, (h + 1) * hw)
            qs = _stack_maps(q[:, sl], qk_dim)
            s = jnp.where(keep, scores(qs, kn[:, sl]), -jnp.inf)
            update(h, s, lambda p, sl=sl: weighted(p, vn[:, sl]))
            o = acc_scr[h] / l_scr[h]
            o_ref[:, sl] = o[:ts] - lam * o[ts:]


def _paged_call(page_table, lam, q, k_new, v_new, cache_k, cache_v, layer, heads, qk_dim, lam_init, ts):
    n, width = q.shape
    nb, n_pages = page_table.shape
    pages = _tile(n_pages, 8)
    n_groups = n_pages // pages
    page = cache_k.shape[-1]
    hw = 2 * qk_dim
    kern = functools.partial(_paged_kernel, pages=pages, heads=heads, qk_dim=qk_dim, lam_init=lam_init,
                             n_groups=n_groups)
    tok = pl.BlockSpec((ts, width), lambda b, g, pt: (b, 0))

    def k_spec(p):
        return pl.BlockSpec((1, 1, heads, 2, qk_dim, page),
                            lambda b, g, pt: (layer, pt[b, g * pages + p], 0, 0, 0, 0))

    def v_spec(p):
        return pl.BlockSpec((1, 1, page * heads, hw), lambda b, g, pt: (layer, pt[b, g * pages + p], 0, 0))

    grid_spec = pltpu.PrefetchScalarGridSpec(
        num_scalar_prefetch=1,
        grid=(nb, n_groups),
        in_specs=[pl.BlockSpec(lam.shape, lambda b, g, pt: (0, 0)), tok, tok, tok]
        + [k_spec(p) for p in range(pages)] + [v_spec(p) for p in range(pages)],
        out_specs=tok,
        scratch_shapes=[pltpu.VMEM((heads, 2 * ts, 1), F32), pltpu.VMEM((heads, 2 * ts, 1), F32),
                        pltpu.VMEM((heads, 2 * ts, hw), F32)],
    )
    return pl.pallas_call(
        kern,
        grid_spec=grid_spec,
        out_shape=jax.ShapeDtypeStruct((n, width), F32),
        compiler_params=_params("parallel", "arbitrary"),
        name="paged_attention",
    )(page_table, lam, q, k_new, v_new, *([cache_k] * pages), *([cache_v] * pages))


def _post_kernel(y_ref, r_ref, k_ref, v_ref, g_ref, o_ref, ga_ref, gb_ref, x_ref, g1_ref, sh2_ref, sc2_ref,
                 lng_ref, lnb_ref, rk_ref, sub_ref, n2_ref, ones_ref, wor_ref, wod_ref, wout_ref, *rest,
                 hs, v_dim, lam_init, n_experts, precise):
    if n_experts:
        router_ref, x_out_ref, h2_ref, gate_ref = rest
    else:
        x_out_ref, h2_ref = rest
    ones = ones_ref[...]
    y = y_ref[...]
    inv_hs = 1.0 / hs
    mu = _dot_exact_rhs(y, ones) * inv_hs
    yc = y - mu
    var = _dot_exact_rhs(yc * yc, ones) * inv_hs
    yn = yc * lax.rsqrt(var + LN_X_EPS) * lng_ref[...] + lnb_ref[...]
    v = v_ref[...]
    bonus = _dot_exact_rhs(r_ref[...] * k_ref[...] * rk_ref[...], ones) * v
    out_a = _mm((yn + bonus) * g_ref[...], wor_ref[...], precise)

    o = o_ref[...]
    parts = []
    for h in range(o.shape[1] // v_dim):
        oh = o[:, h * v_dim:(h + 1) * v_dim]
        parts.append(oh * lax.rsqrt(jnp.mean(oh * oh, axis=-1, keepdims=True) + NORM_EPS))
    on = jnp.concatenate(parts, axis=1) * sub_ref[...] * (1.0 - lam_init)
    out_b = _mm(on, wod_ref[...], precise)

    merged = _sigmoid(ga_ref[...]) * out_a + _sigmoid(gb_ref[...]) * out_b
    x = x_ref[...] + g1_ref[0] * _mm(merged, wout_ref[...], precise)
    x_out_ref[...] = x
    xn = x * lax.rsqrt(jnp.mean(x * x, axis=-1, keepdims=True) + NORM_EPS)
    h2 = (xn * n2_ref[...]) * (1.0 + sc2_ref[0]) + sh2_ref[0]
    h2_ref[...] = h2.astype(h2_ref.dtype)
    if n_experts:
        logits = _dot3(h2, router_ref[...])
        lane = _iota(logits.shape, 1).astype(F32)
        big = float(LANES)
        lg = jnp.where(lane < n_experts, logits, -jnp.inf)
        m1 = jnp.max(lg, axis=-1, keepdims=True)
        i1 = jnp.min(jnp.where(lg == m1, lane, big), axis=-1, keepdims=True)
        lg2 = jnp.where(lane == i1, -jnp.inf, lg)
        m2 = jnp.max(lg2, axis=-1, keepdims=True)
        i2 = jnp.min(jnp.where(lg2 == m2, lane, big), axis=-1, keepdims=True)
        e2 = jnp.exp(m2 - m1)
        den = 1.0 + e2
        gate_ref[...] = jnp.where(lane == i1, 1.0 / den, 0.0) + jnp.where(lane == i2, e2 / den, 0.0)


def _post_call(y, r, k, v, g, o, proj, x, rows, lw, ones_head, gate_cols, lam_init, hs, v_dim):
    n, d = x.shape
    width = y.shape[1]
    tm = _tile(n, 256)
    router = lw.get("router_pad")
    n_experts = lw["n_experts"] if router is not None else 0
    tok = lambda wd: pl.BlockSpec((tm, wd), lambda i: (i, 0))
    row_spec = lambda wd: pl.BlockSpec((1, wd), lambda i: (0, 0))
    full = lambda a: pl.BlockSpec(a.shape, lambda i: (0,) * a.ndim)
    in_specs = [tok(width)] * 6 + [
        pl.BlockSpec((tm, d), lambda i: (i, gate_cols[0])), pl.BlockSpec((tm, d), lambda i: (i, gate_cols[1])),
        tok(d), rows.mod_spec(tm, d, 2, 1), rows.mod_spec(tm, d, 3, 1), rows.mod_spec(tm, d, 4, 1),
        row_spec(width), row_spec(width), row_spec(width), row_spec(width), row_spec(d), full(ones_head),
        full(lw["w_o_rwkv"]), full(lw["w_o_diff"]), full(lw["w_out"]),
    ]
    args = [y, r, k, v, g, o, proj, proj, x, rows.mod3, rows.mod3, rows.mod3,
            lw["ln_x_g"], lw["ln_x_b"], lw["r_k"], lw["subln_g"], lw["norm2_g"], ones_head,
            lw["w_o_rwkv"], lw["w_o_diff"], lw["w_out"]]
    out_specs = [tok(d), tok(d)]
    precise = lw["w_out"].dtype == F32
    out_shape = [jax.ShapeDtypeStruct((n, d), F32), jax.ShapeDtypeStruct((n, d), F32 if precise else BF16)]
    if n_experts:
        in_specs.append(full(router))
        args.append(router)
        out_specs.append(tok(LANES))
        out_shape.append(jax.ShapeDtypeStruct((n, LANES), F32))
    kern = functools.partial(_post_kernel, hs=hs, v_dim=v_dim, lam_init=lam_init, n_experts=n_experts,
                             precise=precise)
    return pl.pallas_call(
        kern,
        grid=(n // tm,),
        in_specs=in_specs,
        out_specs=out_specs,
        out_shape=out_shape,
        compiler_params=_params("parallel"),
        name="merge_post",
    )(*args)


def _ffn_kernel(*refs, moe, precise):
    if moe:
        h_ref, x_ref, g2_ref, gate_ref, w1_ref, w3_ref, w2_ref, o_ref, acc_scr = refs
    else:
        h_ref, x_ref, g2_ref, w1_ref, w3_ref, w2_ref, o_ref, acc_scr = refs
    e = pl.program_id(1)
    f = pl.program_id(2)

    @pl.when((e == 0) & (f == 0))
    def _():
        acc_scr[...] = jnp.zeros(acc_scr.shape, F32)

    h = h_ref[...]
    a = _mm(h, w1_ref[0], precise)
    b = _mm(h, w3_ref[0], precise)
    act = a * _sigmoid(a) * b
    if moe:
        gate = gate_ref[...]
        lane = _iota(gate.shape, 1)
        act = act * jnp.sum(jnp.where(lane == e, gate, 0.0), axis=-1, keepdims=True)
    acc_scr[...] += _mm(act, w2_ref[0], precise)

    @pl.when((e == pl.num_programs(1) - 1) & (f == pl.num_programs(2) - 1))
    def _():
        o_ref[...] = x_ref[...] + g2_ref[0] * acc_scr[...]


def _ffn_call(h2, x, rows, gate, w1, w3, w2, tf):
    n, d = x.shape
    n_exp, _, ff = w1.shape
    tm = _tile(n, 512 if w1.dtype == BF16 else 256)
    moe = gate is not None
    tok = lambda wd: pl.BlockSpec((tm, wd), lambda i, e, f: (i, 0))
    in_specs = [tok(d), tok(d), rows.mod_spec(tm, d, 5, 3)]
    args = [h2, x, rows.mod3]
    if moe:
        in_specs.append(tok(LANES))
        args.append(gate)
    in_specs += [pl.BlockSpec((1, d, tf), lambda i, e, f: (e, 0, f)),
                 pl.BlockSpec((1, d, tf), lambda i, e, f: (e, 0, f)),
                 pl.BlockSpec((1, tf, d), lambda i, e, f: (e, f, 0))]
    args += [w1, w3, w2]
    return pl.pallas_call(
        functools.partial(_ffn_kernel, moe=moe, precise=w1.dtype == F32),
        grid=(n // tm, n_exp, ff // tf),
        in_specs=in_specs,
        out_specs=tok(d),
        out_shape=jax.ShapeDtypeStruct((n, d), F32),
        scratch_shapes=[pltpu.VMEM((tm, d), F32)],
        compiler_params=_params("parallel", "arbitrary", "arbitrary"),
        name="moe_ffn" if moe else "dense_ffn",
    )(*args)


def _ff_tile(ff):
    for t in (1408, 896, 512, 256, 128):
        if ff % t == 0 and ff // t >= 2:
            return t
    return ff


MOE_ROW_BLOCK = 144


def _moe_routed_kernel(nblk_ref, h_ref, x_ref, g2_ref, gate_ref, tri_ref, w1_ref, w3_ref, w2_ref, o_ref,
                       acc_scr, slotc_scr, slotr_scr, hsel_scr, ysel_scr, gsel_scr, *, rb, n_exp):
    i = pl.program_id(0)
    e = pl.program_id(1)
    f = pl.program_id(2)
    last_f = pl.num_programs(2) - 1
    tm = h_ref.shape[0]
    nb = nblk_ref[i * n_exp + e]
    dot = functools.partial(jnp.dot, preferred_element_type=F32)

    @pl.when((e == 0) & (f == 0))
    def _():
        acc_scr[...] = jnp.zeros(acc_scr.shape, F32)
        routed = gate_ref[...] > 0.0
        ind = jnp.where(routed, 1.0, 0.0).astype(BF16)
        tri = tri_ref[...]
        rank = dot(tri, ind)
        slotc_scr[...] = jnp.where(routed, rank, -1.0)
        eye = (_iota((LANES, LANES), 0) == _iota((LANES, LANES), 1)).astype(BF16)
        ind_t = _dot_nt(eye, ind)
        rank_t = _dot_nt(ind_t.astype(BF16), tri)
        slotr_scr[...] = jnp.where(ind_t > 0.5, rank_t, -1.0)

    @pl.when(f == 0)
    def _():
        slot_row = slotr_scr[pl.ds(e, 1), :]

        def select(b, carry):
            r0 = pl.multiple_of(b * rb, rb)
            want = (r0 + _iota((rb, 1), 0)).astype(F32)
            onehot = jnp.where(slot_row == want, 1.0, 0.0).astype(BF16)
            hsel_scr[pl.ds(r0, rb), :] = dot(onehot, h_ref[...]).astype(BF16)
            g = _dot_exact_lhs(onehot, gate_ref[...])
            ge = jnp.sum(jnp.where(_iota(g.shape, 1) == e, g, 0.0), axis=-1, keepdims=True)
            gsel_scr[pl.ds(r0, rb), :] = jnp.broadcast_to(ge, (rb, LANES))
            ysel_scr[pl.ds(r0, rb), :] = jnp.zeros((rb, ysel_scr.shape[1]), F32)
            return carry

        lax.fori_loop(0, nb, select, 0)

    def expert(b, carry):
        r0 = pl.multiple_of(b * rb, rb)
        hs = hsel_scr[pl.ds(r0, rb), :]
        a = dot(hs, w1_ref[0])
        act = a * _sigmoid(a) * dot(hs, w3_ref[0]) * gsel_scr[pl.ds(r0, rb), 0:1]
        ysel_scr[pl.ds(r0, rb), :] += dot(act.astype(BF16), w2_ref[0])
        return carry

    lax.fori_loop(0, nb, expert, 0)

    @pl.when(f == last_f)
    def _():
        slotc = slotc_scr[...]
        slot_col = jnp.sum(jnp.where(_iota(slotc.shape, 1) == e, slotc, 0.0), axis=-1, keepdims=True)

        def scatter(b, carry):
            r0 = pl.multiple_of(b * rb, rb)
            want = (r0 + _iota((1, rb), 1)).astype(F32)
            onehot_t = jnp.where(slot_col == want, 1.0, 0.0).astype(BF16)
            acc_scr[...] += dot(onehot_t, ysel_scr[pl.ds(r0, rb), :].astype(BF16))
            return carry

        lax.fori_loop(0, nb, scatter, 0)

    @pl.when((e == n_exp - 1) & (f == last_f))
    def _():
        o_ref[...] = x_ref[...] + g2_ref[0] * acc_scr[...]


def _moe_routed_call(h2, x, rows, gate, w1, w3, w2, tf):
    n, d = x.shape
    n_exp, _, ff = w1.shape
    rb = MOE_ROW_BLOCK
    tm = _tile(n, 1024)
    n_tiles = n // tm
    sel_rows = -(-tm // rb) * rb
    counts = jnp.sum((gate[:, :n_exp] > 0.0).reshape(n_tiles, tm, n_exp).astype(jnp.int32), axis=1)
    nblk = ((counts + rb - 1) // rb).reshape(-1)
    t = jnp.arange(tm)
    tri = (t[None, :] < t[:, None]).astype(BF16)
    tok = lambda wd: pl.BlockSpec((tm, wd), lambda i, e, f, nb: (i, 0))
    mod = rows.mod_spec(tm, d, 5, 3)
    grid_spec = pltpu.PrefetchScalarGridSpec(
        num_scalar_prefetch=1,
        grid=(n_tiles, n_exp, ff // tf),
        in_specs=[tok(d), tok(d), pl.BlockSpec(mod.block_shape, lambda i, e, f, nb: mod.index_map(i)), tok(LANES),
                  pl.BlockSpec((tm, tm), lambda i, e, f, nb: (0, 0)),
                  pl.BlockSpec((1, d, tf), lambda i, e, f, nb: (e, 0, f)),
                  pl.BlockSpec((1, d, tf), lambda i, e, f, nb: (e, 0, f)),
                  pl.BlockSpec((1, tf, d), lambda i, e, f, nb: (e, f, 0))],
        out_specs=tok(d),
        scratch_shapes=[pltpu.VMEM((tm, d), F32), pltpu.VMEM((tm, LANES), F32), pltpu.VMEM((LANES, tm), F32),
                        pltpu.VMEM((sel_rows, d), BF16), pltpu.VMEM((sel_rows, d), F32),
                        pltpu.VMEM((sel_rows, LANES), F32)],
    )
    return pl.pallas_call(
        functools.partial(_moe_routed_kernel, rb=rb, n_exp=n_exp),
        grid_spec=grid_spec,
        out_shape=jax.ShapeDtypeStruct((n, d), F32),
        compiler_params=_params("parallel", "arbitrary", "arbitrary"),
        name="moe_routed",
    )(nblk, h2, x, rows.mod3, gate, tri, w1, w3, w2)


def _block_ones(width, seg):
    i = jnp.arange(width) // seg
    return (i[:, None] == i[None, :]).astype(BF16)


def _rope_tables(pos, qk_dim, n_seg):
    half = qk_dim // 2
    inv_freq = ROPE_THETA ** (-jnp.arange(half, dtype=F32) / half)
    ang = pos.astype(F32)[:, None] * inv_freq[None, :]
    cos, sin = jnp.cos(ang), jnp.sin(ang)
    cos_t = jnp.tile(jnp.concatenate([cos, cos], axis=1), (1, n_seg))
    sin_t = jnp.tile(jnp.concatenate([-sin, sin], axis=1), (1, n_seg))
    return cos_t, sin_t


def kernel(x_prompt, x_sample, c_prompt, c_sample, cache_k, cache_v, state_wkv, state_shift, page_table,
           norm1_g, norm2_g, w_ada, b_ada, w_in, mu_shift, w0, w_up, a0, a_up, g_up, k_k, k_a, r_k,
           ln_x_g, ln_x_b, w_o_rwkv, q_norm_g, k_norm_g, lam, subln_g, w_o_diff, w_out,
           ffn_w1, ffn_w3, ffn_w2, router, moe_w1, moe_w3, moe_w2):
    n_batch, seq, d = x_prompt.shape
    n_dec, dec_seq, _ = x_sample.shape
    depth = w_in.shape[0]
    heads_r, hs = r_k.shape[1], r_k.shape[2]
    width_r = heads_r * hs
    proj_r = mu_shift.shape[-1]
    n_phys, page, heads_d, _, qk_dim = cache_k.shape[1:]
    v_dim = cache_v.shape[-1]
    qk_w = heads_d * 2 * qk_dim
    v_w = heads_d * v_dim
    w_lora, a_lora = w_up.shape[1], a_up.shape[1]
    assert v_dim == 2 * qk_dim == LANES and qk_w == v_w and w_lora + a_lora == LANES
    past_len = page_table.shape[1] * page
    n_p, n_s = n_batch * seq, n_dec * dec_seq

    o_q = proj_r
    o_k, o_v, o_g = o_q + qk_w, o_q + 2 * qk_w, o_q + 2 * qk_w + v_w
    w_in_f = jnp.concatenate([w_in[:, :, o_g:], w_in[:, :, o_q:o_g], w_in[:, :, :o_q]], axis=-1)
    w_in_b = w_in_f.astype(BF16)
    n_gate = w_in.shape[-1] - o_g
    assert n_gate == 2 * d and (n_gate + 3 * qk_w) % proj_r == 0 and n_gate % qk_w == 0
    gate_cols = (0, 1)
    qkv_cols = tuple(n_gate // qk_w + j for j in range(3))
    rwkv_col = (n_gate + 3 * qk_w) // proj_r
    tn = proj_r

    c_all = jnp.concatenate([c_prompt, c_sample], axis=0)
    c_all = jnp.pad(c_all, ((0, (-c_all.shape[0]) % 8), (0, 0)))
    mod = _ada_call(c_all, w_ada, b_ada)

    ones_head = _block_ones(width_r, hs)
    ones_qk = _block_ones(qk_w, qk_dim)
    cos_p, sin_p = _rope_tables(jnp.arange(seq), qk_dim, qk_w // qk_dim)
    cos_s, sin_s = _rope_tables(past_len + jnp.arange(dec_seq), qk_dim, qk_w // qk_dim)
    cos_s, sin_s = jnp.tile(cos_s, (n_dec, 1)), jnp.tile(sin_s, (n_dec, 1))
    cache_kt = jnp.transpose(cache_k, (0, 1, 3, 4, 5, 2))
    cache_v4 = cache_v.reshape(depth, n_phys, page * heads_d, v_dim)

    zpad = lambda m, before, after: jnp.pad(m, ((before, after), (0, 0)))
    bf = lambda a: a.astype(BF16)
    xp = x_prompt.reshape(n_p, d)
    xs = x_sample.reshape(n_s, d)
    shift0 = jnp.zeros((n_batch, 1, proj_r), F32)
    wkv_zero = jnp.zeros((n_batch, heads_r, hs, hs), F32)
    wkv_chunk = _tile(seq, 64)
    dec_chunk = max(16, dec_seq)
    outs = {key: [] for key in ("kp", "vp", "ks", "vs", "wp", "ws", "sp", "ss")}

    for l in range(depth):
        lam_init = 0.8 - 0.6 * math.exp(-0.3 * l)
        lw = {
            "mu_shift": mu_shift[l][None], "w0": w0[l][None], "a0": a0[l][None],
            "w_up_pad": zpad(w_up[l], 0, a_lora), "a_up_pad": zpad(a_up[l], w_lora, 0), "g_up": g_up[l],
            "k_k": k_k[l][None], "k_a": k_a[l][None], "r_k": r_k[l].reshape(1, width_r),
            "ln_x_g": ln_x_g[l][None], "ln_x_b": ln_x_b[l][None],
            "subln_g": jnp.tile(subln_g[l], heads_d)[None], "norm2_g": norm2_g[l][None],
        }
        qg = jnp.tile(q_norm_g[l].reshape(-1), heads_d)[None]
        kg = jnp.tile(k_norm_g[l].reshape(-1), heads_d)[None]
        if l % 2 == 0:
            i = l // 2
            ffn_f = (ffn_w1[i][None], ffn_w3[i][None], ffn_w2[i][None])
        else:
            i = l // 2
            n_experts = router.shape[-1]
            lw["router_pad"] = jnp.pad(router[i], ((0, 0), (0, LANES - n_experts)))
            lw["n_experts"] = n_experts
            ffn_f = (moe_w1[i], moe_w3[i], moe_w2[i])
        tf = _ff_tile(ffn_f[0].shape[-1])
        norm1 = norm1_g[l][None]
        out_w = {"w_o_rwkv": w_o_rwkv[l], "w_o_diff": w_o_diff[l], "w_out": w_out[l]}
        lw_p = dict(lw, **{key: bf(val) for key, val in out_w.items()})
        lw_s = dict(lw, **out_w)
        ffn_b = tuple(bf(w) for w in ffn_f)

        def run_group(x, rows, init_rows, state0, is_prompt):
            lw_g, w_in_g, ffn = (lw_p, w_in_b[l], ffn_b) if is_prompt else (lw_s, w_in_f[l], ffn_f)
            proj = _proj_call(x, rows, norm1, w_in_g, tn)
            r, ld, k2, v_r, av, bv, g = _rwkv_prep_call(proj, rows, init_rows, lw_g, rwkv_col, ones_head,
                                                        not is_prompt)
            if is_prompt:
                y, state = _wkv_call(r, ld, k2, v_r, av, bv, state0, rows.seq_len, wkv_chunk, False,
                                     n_sub=WKV_CHUNKS_PER_STEP if seq % (wkv_chunk * WKV_CHUNKS_PER_STEP) == 0 else 1)
                k_rows, v_rows, q_bf, k_bf, v_aug = _attn_prep_call(
                    proj, rows, cos_p, sin_p, qg, kg, ones_qk, qkv_cols, BF16, qk_dim)
                o = _flash_call(q_bf, k_bf, v_aug, lam[l], n_batch, seq, heads_d, qk_dim, lam_init)
            else:
                padt = lambda a: jnp.pad(a.reshape(n_dec, dec_seq, width_r),
                                         ((0, 0), (0, dec_chunk - dec_seq), (0, 0))).reshape(-1, width_r)
                y, state = _wkv_call(padt(r), padt(ld), padt(k2), padt(v_r), padt(av), padt(bv), state0,
                                     dec_chunk, dec_chunk, True)
                y = y.reshape(n_dec, dec_chunk, width_r)[:, :dec_seq].reshape(-1, width_r)
                k_rows, v_rows, q_f, _, _ = _attn_prep_call(
                    proj, rows, cos_s, sin_s, qg, kg, ones_qk, qkv_cols, F32, qk_dim)
                o = _paged_call(page_table, lam[l], q_f, k_rows, v_rows, cache_kt, cache_v4, l, heads_d,
                                qk_dim, lam_init, dec_seq)
            post = _post_call(y, r, k2, v_r, g, o, proj, x, rows, lw_g, ones_head, gate_cols, lam_init, hs, v_dim)
            x_new, h2 = post[0], post[1]
            gate = post[2] if len(post) > 2 else None
            ffn_call = _moe_routed_call if (is_prompt and gate is not None) else _ffn_call
            x_out = ffn_call(h2, x_new, rows, gate, *ffn, tf)
            shift = proj.reshape(-1, rows.seq_len, proj.shape[-1])[:, -1, rwkv_col * proj_r:]
            return x_out, k_rows, v_rows, state, shift

        rows_p = _Rows(n_p, seq, mod[l, :n_batch][:, None, :])
        rows_s = _Rows(n_s, dec_seq, jnp.repeat(mod[l, n_batch:n_batch + n_dec], dec_seq, axis=0)[None])
        xp, kr, vr, st, sh = run_group(xp, rows_p, shift0, wkv_zero, True)
        outs["kp"].append(kr.reshape(n_batch, seq, heads_d, 2, qk_dim))
        outs["vp"].append(vr.reshape(n_batch, seq, heads_d, v_dim))
        outs["wp"].append(st)
        outs["sp"].append(sh)
        init_s = jnp.repeat(state_shift[l], dec_seq, axis=0)
        xs, kr, vr, st, sh = run_group(xs, rows_s, init_s, state_wkv[l], False)
        outs["ks"].append(kr.reshape(n_dec, dec_seq, heads_d, 2, qk_dim))
        outs["vs"].append(vr.reshape(n_dec, dec_seq, heads_d, v_dim))
        outs["ws"].append(st)
        outs["ss"].append(sh)

    st = lambda key: jnp.stack(outs[key])
    return (xp.reshape(n_batch, seq, d), xs.reshape(n_dec, dec_seq, d), st("kp"), st("vp"), st("ks"), st("vs"),
            st("wp"), st("ws"), st("sp"), st("ss"))
```

```python
import functools
import math

import jax
import jax.numpy as jnp
from jax import lax
from jax.experimental import pallas as pl
from jax.experimental.pallas import tpu as pltpu

F32 = jnp.float32
BF16 = jnp.bfloat16

NORM_EPS = 1e-6
LN_X_EPS = 64e-5
ROPE_THETA = 10000.0
TOP_K = 2
KK_NORM_FLOOR = 1e-12
LANES = 128
VMEM_LIMIT = 56 * 1024 * 1024
FLASH_ROW_PARTS = 4
WKV_CHUNKS_PER_STEP = 4


def _params(*sem):
    return pltpu.CompilerParams(dimension_semantics=sem, vmem_limit_bytes=VMEM_LIMIT)


def _tile(n, pref):
    if n <= pref:
        return n
    t = pref
    while n % t:
        t //= 2
    return t


def _bdot(a, b):
    return jnp.dot(a.astype(BF16), b.astype(BF16), preferred_element_type=F32)


def _mm(a, b, precise):
    return _dot3(a.astype(F32), b.astype(F32)) if precise else _bdot(a, b)


def _dot_nt(a, b):
    return lax.dot_general(a, b, (((1,), (1,)), ((), ())), preferred_element_type=F32)


def _split2(x):
    hi = x.astype(BF16)
    lo = (x - hi.astype(F32)).astype(BF16)
    return hi, lo


def _split3(x):
    hi = x.astype(BF16)
    r1 = x - hi.astype(F32)
    mid = r1.astype(BF16)
    lo = (r1 - mid.astype(F32)).astype(BF16)
    return hi, mid, lo


def _dot3(a, b):
    ah, al = _split2(a)
    bh, bl = _split2(b)
    d = functools.partial(jnp.dot, preferred_element_type=F32)
    return d(ah, bh) + d(ah, bl) + d(al, bh)


def _dot3_nt(a, b):
    ah, al = _split2(a)
    bh, bl = _split2(b)
    return _dot_nt(ah, bh) + _dot_nt(ah, bl) + _dot_nt(al, bh)


def _dot_exact_rhs(x, m_bf):
    hi, mid, lo = _split3(x)
    d = functools.partial(jnp.dot, preferred_element_type=F32)
    return d(hi, m_bf) + d(mid, m_bf) + d(lo, m_bf)


def _dot_exact_lhs(m_bf, x):
    hi, mid, lo = _split3(x)
    d = functools.partial(jnp.dot, preferred_element_type=F32)
    return d(m_bf, hi) + d(m_bf, mid) + d(m_bf, lo)


def _sigmoid(x):
    return 1.0 / (1.0 + jnp.exp(-x))


def _softplus(x):
    return jnp.maximum(x, 0.0) + jnp.log(1.0 + jnp.exp(-jnp.abs(x)))


def _iota(shape, dim):
    return lax.broadcasted_iota(jnp.int32, shape, dim)


def _ada_kernel(c_ref, w_ref, b_ref, o_ref):
    o_ref[0] = _dot3(c_ref[...], w_ref[0]) + b_ref[0]


def _ada_call(c_all, w_ada, b_ada):
    n_layers, d, d6 = w_ada.shape
    m = c_all.shape[0]
    tn = _tile(d6, 1536)
    return pl.pallas_call(
        _ada_kernel,
        grid=(n_layers, d6 // tn),
        in_specs=[
            pl.BlockSpec((m, d), lambda l, j: (0, 0)),
            pl.BlockSpec((1, d, tn), lambda l, j: (l, 0, j)),
            pl.BlockSpec((1, 1, tn), lambda l, j: (l, 0, j)),
        ],
        out_specs=pl.BlockSpec((1, m, tn), lambda l, j: (l, 0, j)),
        out_shape=jax.ShapeDtypeStruct((n_layers, m, d6), F32),
        compiler_params=_params("parallel", "parallel"),
        name="adaln_mod",
    )(c_all, w_ada, b_ada.reshape(n_layers, 1, d6))


class _Rows:
    def __init__(self, n, seq_len, mod3):
        self.n, self.seq_len, self.mod3 = n, seq_len, mod3
        self.per_row = mod3.shape[1] != 1

    def mod_spec(self, tm, d, col, n_grid):
        r = self.mod3.shape[1]
        if self.per_row:
            assert r == tm == self.n
            idx = lambda i, *_: (0, 0, col)
        else:
            assert self.seq_len % tm == 0
            per = self.seq_len // tm
            idx = lambda i, *_: (i // per, 0, col)
        return pl.BlockSpec((1, r, d), idx)


def _proj_kernel(x_ref, sh_ref, sc_ref, g_ref, w_ref, o_ref, h_scr, *, precise):
    @pl.when(pl.program_id(1) == 0)
    def _():
        x = x_ref[...]
        y = x * lax.rsqrt(jnp.mean(x * x, axis=-1, keepdims=True) + NORM_EPS)
        h = (y * g_ref[...]) * (1.0 + sc_ref[0]) + sh_ref[0]
        h_scr[...] = h.astype(h_scr.dtype)

    o_ref[...] = _mm(h_scr[...], w_ref[...], precise)


def _proj_call(x, rows, norm_g, w, tn):
    n, d = x.shape
    pw = w.shape[1]
    tm = _tile(n, 512)
    precise = w.dtype == F32
    return pl.pallas_call(
        functools.partial(_proj_kernel, precise=precise),
        grid=(n // tm, pw // tn),
        in_specs=[
            pl.BlockSpec((tm, d), lambda i, j: (i, 0)),
            rows.mod_spec(tm, d, 0, 2),
            rows.mod_spec(tm, d, 1, 2),
            pl.BlockSpec((1, d), lambda i, j: (0, 0)),
            pl.BlockSpec((d, tn), lambda i, j: (0, j)),
        ],
        out_specs=pl.BlockSpec((tm, tn), lambda i, j: (i, j)),
        out_shape=jax.ShapeDtypeStruct((n, pw), F32),
        scratch_shapes=[pltpu.VMEM((tm, d), w.dtype)],
        compiler_params=_params("parallel", "arbitrary"),
        name="proj_in",
    )(x, rows.mod3, rows.mod3, norm_g, w)


def _rwkv_prep_kernel(cols_ref, prev8_ref, init_ref, mu_ref, w0_ref, wup_ref, a0_ref, aup_ref,
                      gup_ref, kk_ref, ka_ref, ones_ref,
                      r_ref, ld_ref, k_ref, v_ref, av_ref, bv_ref, g_ref, *, seq_len, tm, width, precise):
    cols = cols_ref[...]
    rolled = pltpu.roll(cols, 1, 0)
    row = _iota((tm, 1), 0)
    if seq_len >= tm:
        first_tile = (pl.program_id(0) % (seq_len // tm)) == 0
        boundary = jnp.where(first_tile, init_ref[0], prev8_ref[7:8, :])
        prev = jnp.where(row == 0, boundary, rolled)
    else:
        prev = jnp.where(row % seq_len == 0, init_ref[...], rolled)
    xs = cols + (prev - cols) * mu_ref[...]
    w = width
    r = xs[:, 0:w]
    k = xs[:, w:2 * w]
    v = xs[:, 2 * w:3 * w]
    lora_wa = xs[:, 3 * w:3 * w + LANES]
    gd = xs[:, 3 * w + LANES:]
    w_log = -_softplus(-(w0_ref[...] + _mm(jnp.tanh(lora_wa), wup_ref[...], precise))) - 0.5
    a = _sigmoid(a0_ref[...] + _mm(lora_wa, aup_ref[...], precise))
    g = _mm(_sigmoid(gd), gup_ref[...], precise)
    kk = k * kk_ref[...]
    ss = _dot_exact_rhs(kk * kk, ones_ref[...])
    kk = kk / jnp.maximum(jnp.sqrt(ss), KK_NORM_FLOOR)
    r_ref[...] = r
    ld_ref[...] = -jnp.exp(w_log)
    k_ref[...] = k * (1.0 + (a - 1.0) * ka_ref[...])
    v_ref[...] = v
    av_ref[...] = -kk
    bv_ref[...] = kk * a
    g_ref[...] = g


def _rwkv_prep_call(proj, rows, init_rows, lw, col_block, ones_head, precise):
    n = proj.shape[0]
    width = lw["w0"].shape[-1]
    pw = lw["mu_shift"].shape[-1]
    tm = _tile(n, 256) if rows.seq_len >= 256 else n
    assert pw == 3 * width + LANES + lw["g_up"].shape[0]
    if rows.seq_len >= tm:
        init_spec = pl.BlockSpec((1, 1, pw), lambda i: ((i * tm) // rows.seq_len, 0, 0))
    else:
        init_spec = pl.BlockSpec((tm, pw), lambda i: (0, 0))
    row_spec = lambda wd: pl.BlockSpec((1, wd), lambda i: (0, 0))
    full = lambda a: pl.BlockSpec(a.shape, lambda i: (0,) * a.ndim)
    out = jax.ShapeDtypeStruct((n, width), F32)
    kern = functools.partial(_rwkv_prep_kernel, seq_len=rows.seq_len, tm=tm, width=width, precise=precise)
    return pl.pallas_call(
        kern,
        grid=(n // tm,),
        in_specs=[
            pl.BlockSpec((tm, pw), lambda i: (i, col_block)),
            pl.BlockSpec((8, pw), lambda i: (jnp.maximum(i * (tm // 8) - 1, 0), col_block)),
            init_spec,
            row_spec(pw), row_spec(width), full(lw["w_up_pad"]), row_spec(width), full(lw["a_up_pad"]),
            full(lw["g_up"]), row_spec(width), row_spec(width), full(ones_head),
        ],
        out_specs=[pl.BlockSpec((tm, width), lambda i: (i, 0))] * 7,
        out_shape=[out] * 7,
        compiler_params=_params("parallel"),
        name="rwkv_prep",
    )(proj, proj, init_rows, lw["mu_shift"], lw["w0"], lw["w_up_pad"], lw["a0"], lw["a_up_pad"],
      lw["g_up"], lw["k_k"], lw["k_a"], ones_head)


def _wkv_kernel(r_ref, ld_ref, k_ref, v_ref, a_ref, b_ref, s0_ref, y_ref, s_ref, *, chunk, heads, hs, precise):
    c = chunk
    if precise:
        mm, mm_nt = _dot3, _dot3_nt
    else:
        mm = _bdot
        mm_nt = lambda a, b: _dot_nt(a.astype(BF16), b.astype(BF16))

    @pl.when(pl.program_id(1) == 0)
    def _():
        s_ref[...] = s0_ref[...]

    rr = _iota((c, c), 0)
    cc = _iota((c, c), 1)
    strict = rr > cc
    incl = rr >= cc
    tri = incl.astype(BF16)
    eye_c = (rr == cc).astype(F32)
    eye_v = (_iota((hs, hs), 0) == _iota((hs, hs), 1)).astype(BF16)

    n_double = int(math.log2(c)) - 1
    hr = range(heads)
    sls = [slice(h * hs, (h + 1) * hs) for h in hr]
    n_sub = r_ref.shape[0] // c
    items = [(ci, h) for ci in range(n_sub) for h in hr]

    pre = []
    for ci in range(n_sub):
        rows = slice(ci * c, (ci + 1) * c)
        ld = ld_ref[rows, :]
        cum = _dot_exact_lhs(tri, ld)
        tot = cum[c - 1:c, :]
        e_neg = jnp.exp(-cum)
        e_dec = jnp.exp(tot - cum)
        b_all, k_all = b_ref[rows, :], k_ref[rows, :]
        pre.append(dict(a_t=a_ref[rows, :] * jnp.exp(cum - ld), r_t=r_ref[rows, :] * jnp.exp(cum),
                        b_t=b_all * e_neg, k_t=k_all * e_neg, b_d=b_all * e_dec, k_d=k_all * e_dec,
                        e_tot=jnp.exp(tot), v=v_ref[rows, :]))
    get = lambda name, ci, h: pre[ci][name][:, sls[h]]

    pm = [mm_nt(jnp.concatenate([get("a_t", ci, h), get("r_t", ci, h)], axis=0),
                jnp.concatenate([get("b_t", ci, h), get("k_t", ci, h)], axis=0)) for ci, h in items]
    l_ab = [jnp.where(strict, p[:c, :c], 0.0) for p in pm]
    l_ak = [jnp.where(strict, p[:c, c:], 0.0) for p in pm]
    m_rb = [jnp.where(incl, p[c:, :c], 0.0) for p in pm]
    m_rk = [jnp.where(incl, p[c:, c:], 0.0) for p in pm]
    t_inv = [eye_c + l for l in l_ab]
    l_pow = l_ab
    for _ in range(n_double):
        l_pow = [mm(l, l) for l in l_pow]
        t_inv = [t + mm(t, l) for t, l in zip(t_inv, l_pow)]
    vh = [get("v", ci, h) for ci, h in items]
    z = [mm(l_ak[j], vh[j]) for j in range(len(items))]
    au = [mm(t_inv[j], jnp.concatenate([get("a_t", ci, h), z[j]], axis=1)) for j, (ci, h) in enumerate(items)]
    state = [s_ref[0, h] for h in hr]
    for ci in range(n_sub):
        rows = slice(ci * c, (ci + 1) * c)
        for h in hr:
            j = ci * heads + h
            s0 = state[h]
            u = mm_nt(au[j][:, :hs], s0) + au[j][:, hs:]
            y = mm_nt(get("r_t", ci, h), s0) + mm(m_rb[j], u) + mm(m_rk[j], vh[j])
            uv = jnp.concatenate([u, vh[j]], axis=0)
            if precise:
                uh, um, ul = _split3(uv)
                uv_t = _dot_nt(eye_v, uh) + _dot_nt(eye_v, um) + _dot_nt(eye_v, ul)
            else:
                uv_t = _dot_nt(eye_v, uv.astype(BF16))
            bkd = jnp.concatenate([get("b_d", ci, h), get("k_d", ci, h)], axis=0)
            state[h] = s0 * pre[ci]["e_tot"][:, sls[h]] + mm(uv_t, bkd)
            y_ref[rows, sls[h]] = y
    for h in hr:
        s_ref[0, h] = state[h]


def _wkv_call(r, ld, k, v, av, bv, state0, seq_len, chunk, precise, n_sub=1):
    n, width = r.shape
    nb, heads, hs, _ = state0.shape
    n_chunks = seq_len // (chunk * n_sub)
    tok = pl.BlockSpec((chunk * n_sub, width), lambda b, c: (b * n_chunks + c, 0))
    st = pl.BlockSpec((1, heads, hs, hs), lambda b, c: (b, 0, 0, 0))
    kern = functools.partial(_wkv_kernel, chunk=chunk, heads=heads, hs=hs, precise=precise)
    return pl.pallas_call(
        kern,
        grid=(nb, n_chunks),
        in_specs=[tok] * 6 + [st],
        out_specs=[tok, st],
        out_shape=[jax.ShapeDtypeStruct((n, width), F32), jax.ShapeDtypeStruct(state0.shape, F32)],
        compiler_params=_params("parallel", "arbitrary"),
        name="wkv_scan",
    )(r, ld, k, v, av, bv, state0)


def _attn_prep_kernel(q_ref, k_ref, v_ref, cos_ref, sin_ref, qg_ref, kg_ref, ones_ref,
                      kr_ref, vr_ref, qo_ref, kb_ref, vb_ref, *, qk_dim, scale):
    ones = ones_ref[...]
    cos = cos_ref[...]
    sin = sin_ref[...]
    lane = _iota(cos.shape, 1)
    first_half = (lane % qk_dim) < (qk_dim // 2)
    width = cos.shape[1]
    half = qk_dim // 2

    def norm_rope(x, g):
        ms = _dot_exact_rhs(x * x, ones) * (1.0 / qk_dim)
        y = x * lax.rsqrt(ms + NORM_EPS) * g
        swapped = jnp.where(first_half, pltpu.roll(y, width - half, 1), pltpu.roll(y, half, 1))
        return y * cos + swapped * sin

    q = norm_rope(q_ref[...], qg_ref[...])
    k = norm_rope(k_ref[...], kg_ref[...])
    v = v_ref[...]
    kr_ref[...] = k
    vr_ref[...] = v
    qo_ref[...] = (q * scale).astype(qo_ref.dtype)
    kb_ref[...] = k.astype(BF16)
    vb = v.astype(BF16)
    hw = 2 * qk_dim
    one = jnp.ones((vb.shape[0], hw), BF16)
    vb_ref[...] = jnp.concatenate(
        [blk for h in range(width // hw) for blk in (vb[:, h * hw:(h + 1) * hw], one)], axis=1)


def _attn_prep_call(proj, rows, cos, sin, qg, kg, ones_qk, col_blocks, q_dtype, qk_dim):
    n = proj.shape[0]
    width = qg.shape[-1]
    tm = _tile(n, 512) if rows.seq_len >= 512 else n
    per = max(rows.seq_len // tm, 1)
    tab = pl.BlockSpec((tm, width), lambda i: (i % per, 0))
    col = lambda cb: pl.BlockSpec((tm, width), lambda i: (i, cb))
    row_spec = pl.BlockSpec((1, width), lambda i: (0, 0))
    out_spec = pl.BlockSpec((tm, width), lambda i: (i, 0))
    kern = functools.partial(_attn_prep_kernel, qk_dim=qk_dim, scale=qk_dim ** -0.5)
    sds = lambda dt: jax.ShapeDtypeStruct((n, width), dt)
    return pl.pallas_call(
        kern,
        grid=(n // tm,),
        in_specs=[col(col_blocks[0]), col(col_blocks[1]), col(col_blocks[2]), tab, tab, row_spec, row_spec,
                  pl.BlockSpec(ones_qk.shape, lambda i: (0, 0))],
        out_specs=[out_spec] * 4 + [pl.BlockSpec((tm, 2 * width), lambda i: (i, 0))],
        out_shape=[sds(F32), sds(F32), sds(q_dtype), sds(BF16), jax.ShapeDtypeStruct((n, 2 * width), BF16)],
        compiler_params=_params("parallel"),
        name="attn_prep",
    )(proj, proj, proj, cos, sin, qg, kg, ones_qk)


def _lambda(lam_ref, lam_init):
    lp = lam_ref[...]
    s1 = jnp.sum(lp[0:1] * lp[1:2], axis=-1, keepdims=True)
    s2 = jnp.sum(lp[2:3] * lp[3:4], axis=-1, keepdims=True)
    return jnp.exp(s1) - jnp.exp(s2) + lam_init


def _stack_maps(qh, qk_dim):
    lane = _iota(qh.shape, 1)
    zero = jnp.zeros_like(qh)
    return jnp.concatenate([jnp.where(lane < qk_dim, qh, zero), jnp.where(lane >= qk_dim, qh, zero)], axis=0)


def _flash_kernel(lam_ref, q_ref, k_ref, v_ref, o_ref, qs_scr, m_scr, acc_scr, *, tq, tk, qk_dim, lam_init,
                  n_parts):
    i = pl.program_id(2)
    rows = 2 * tq
    hw = 2 * qk_dim
    n_blk = tk // hw
    qs_scr[...] = _stack_maps(q_ref[...], qk_dim)
    m_scr[...] = jnp.full(m_scr.shape, -jnp.inf, F32)
    acc_scr[...] = jnp.zeros(acc_scr.shape, F32)
    n_full = (i * tq) // tk

    part = rows // n_parts

    def step(j, masked):
        start = pl.multiple_of(j * tk, tk)
        kc = k_ref[pl.ds(start, tk), :]
        vc = v_ref[pl.ds(start, tk), :]
        for r0 in range(0, rows, part):
            rs = slice(r0, r0 + part)
            s = _dot_nt(qs_scr[rs, :], kc)
            if masked:
                qpos = i * tq + (r0 % tq) + _iota((part, tk), 0)
                kpos = start + _iota((part, tk), 1)
                s = jnp.where(kpos <= qpos, s, -jnp.inf)
            blocks = [s[:, c * hw:(c + 1) * hw] for c in range(n_blk)]
            blk_max = blocks[0]
            for blk in blocks[1:]:
                blk_max = jnp.maximum(blk_max, blk)
            m_prev = m_scr[rs, :]
            m_new = jnp.maximum(m_prev, jnp.max(blk_max, axis=-1, keepdims=True))
            alpha = jnp.exp(m_prev - m_new)
            p = jnp.concatenate([jnp.exp(blk - m_new).astype(BF16) for blk in blocks], axis=1)
            pv = jnp.dot(p, vc, preferred_element_type=F32)
            acc_scr[rs, :] = jnp.concatenate([alpha, alpha], axis=1) * acc_scr[rs, :] + pv
            m_scr[rs, :] = m_new

    def body(j, carry):
        step(j, False)
        return carry

    lax.fori_loop(0, n_full, body, 0)
    step(n_full, True)
    acc = acc_scr[...]
    o = acc[:, :hw] / acc[:, hw:]
    lam = _lambda(lam_ref, lam_init)
    o_ref[...] = o[:tq] - lam * o[tq:]


def _flash_call(q_bf, k_bf, v_aug, lam, n_batch, seq_len, heads, qk_dim, lam_init):
    n, width = q_bf.shape
    hw = width // heads
    tq = _tile(seq_len, 512)
    tk = _tile(seq_len, 1024)
    nq = seq_len // tq
    kern = functools.partial(_flash_kernel, tq=tq, tk=tk, qk_dim=qk_dim, lam_init=lam_init,
                             n_parts=FLASH_ROW_PARTS)
    qspec = pl.BlockSpec((tq, hw), lambda b, h, i: (b * nq + i, h))
    return pl.pallas_call(
        kern,
        grid=(n_batch, heads, nq),
        in_specs=[pl.BlockSpec(lam.shape, lambda b, h, i: (0, 0)), qspec,
                  pl.BlockSpec((seq_len, hw), lambda b, h, i: (b, h)),
                  pl.BlockSpec((seq_len, 2 * hw), lambda b, h, i: (b, h))],
        out_specs=qspec,
        out_shape=jax.ShapeDtypeStruct((n, width), F32),
        scratch_shapes=[pltpu.VMEM((2 * tq, hw), BF16), pltpu.VMEM((2 * tq, hw), F32),
                        pltpu.VMEM((2 * tq, 2 * hw), F32)],
        compiler_params=_params("parallel", "parallel", "arbitrary"),
        name="prompt_attention",
    )(lam, q_bf, k_bf, v_aug)


def _paged_kernel(pt_ref, lam_ref, q_ref, kn_ref, vn_ref, *refs, pages, heads, qk_dim, lam_init, n_groups):
    del pt_ref
    k_refs = refs[:pages]
    v_refs = refs[pages:2 * pages]
    o_ref = refs[2 * pages]
    m_scr, l_scr, acc_scr = refs[2 * pages + 1:]
    g = pl.program_id(1)
    hw = 2 * qk_dim
    ts = q_ref.shape[0]

    @pl.when(g == 0)
    def _():
        m_scr[...] = jnp.full(m_scr.shape, -jnp.inf, F32)
        l_scr[...] = jnp.zeros(l_scr.shape, F32)
        acc_scr[...] = jnp.zeros(acc_scr.shape, F32)

    q = q_ref[...]

    def update(h, s, pv_fn):
        m_prev = m_scr[h]
        m_new = jnp.maximum(m_prev, jnp.max(s, axis=-1, keepdims=True))
        alpha = jnp.exp(m_prev - m_new)
        p = jnp.exp(s - m_new)
        l_scr[h] = alpha * l_scr[h] + jnp.sum(p, axis=-1, keepdims=True)
        acc_scr[h] = alpha * acc_scr[h] + pv_fn(p)
        m_scr[h] = m_new

    def scores(qs, keys):
        qh, ql = _split2(qs)
        kh, kl = _split2(keys)
        both = _dot_nt(jnp.concatenate([qh, ql], axis=0), kh)
        return both[:2 * ts] + both[2 * ts:] + _dot_nt(qh, kl)

    def weighted(p, vals):
        ph, plo = _split2(p)
        vh, vl = _split2(vals)
        both = jnp.dot(jnp.concatenate([ph, plo], axis=0), vh, preferred_element_type=F32)
        return both[:2 * ts] + both[2 * ts:] + jnp.dot(ph, vl, preferred_element_type=F32)

    def scores_t(qs, keys_t):
        qh, ql = _split2(qs)
        kh, kl = _split2(keys_t)
        both = jnp.dot(jnp.concatenate([qh, ql], axis=0), kh, preferred_element_type=F32)
        return both[:2 * ts] + both[2 * ts:] + jnp.dot(qh, kl, preferred_element_type=F32)

    page = k_refs[0].shape[-1]
    hr = range(heads)
    qs = [_stack_maps(q[:, h * hw:(h + 1) * hw], qk_dim) for h in hr]
    keys_t = [jnp.concatenate([k_refs[p][0, 0, h].reshape(hw, page) for p in range(pages)], axis=1) for h in hr]
    s = [scores_t(qs[h], keys_t[h]) for h in hr]
    m_prev = [m_scr[h] for h in hr]
    m_new = [jnp.maximum(m_prev[h], jnp.max(s[h], axis=-1, keepdims=True)) for h in hr]
    alpha = [jnp.exp(m_prev[h] - m_new[h]) for h in hr]
    prob = [jnp.exp(s[h] - m_new[h]) for h in hr]
    vals = [jnp.concatenate([v_refs[p][0, 0, pl.ds(h, page, stride=heads), :] for p in range(pages)], axis=0)
            for h in hr]
    pv = [weighted(prob[h], vals[h]) for h in hr]
    for h in hr:
        l_scr[h] = alpha[h] * l_scr[h] + jnp.sum(prob[h], axis=-1, keepdims=True)
        acc_scr[h] = alpha[h] * acc_scr[h] + pv[h]
        m_scr[h] = m_new[h]

    @pl.when(g == n_groups - 1)
    def _():
        lam = _lambda(lam_ref, lam_init)
        pad = jnp.zeros((LANES - ts, kn_ref.shape[1]), F32)
        kn = jnp.concatenate([kn_ref[...], pad], axis=0)
        vn = jnp.concatenate([vn_ref[...], pad], axis=0)
        qpos = _iota((2 * ts, LANES), 0) % ts
        kpos = _iota((2 * ts, LANES), 1)
        keep = kpos <= qpos
        for h in range(heads):
            sl = slice(h * hw, (h + 1) * hw)
            qs = _stack_maps(q[:, sl], qk_dim)
            s = jnp.where(keep, scores(qs, kn[:, sl]), -jnp.inf)
            update(h, s, lambda p, sl=sl: weighted(p, vn[:, sl]))
            o = acc_scr[h] / l_scr[h]
            o_ref[:, sl] = o[:ts] - lam * o[ts:]


def _paged_call(page_table, lam, q, k_new, v_new, cache_k, cache_v, layer, heads, qk_dim, lam_init, ts):
    n, width = q.shape
    nb, n_pages = page_table.shape
    pages = _tile(n_pages, 8)
    n_groups = n_pages // pages
    page = cache_k.shape[-1]
    hw = 2 * qk_dim
    kern = functools.partial(_paged_kernel, pages=pages, heads=heads, qk_dim=qk_dim, lam_init=lam_init,
                             n_groups=n_groups)
    tok = pl.BlockSpec((ts, width), lambda b, g, pt: (b, 0))

    def k_spec(p):
        return pl.BlockSpec((1, 1, heads, 2, qk_dim, page),
                            lambda b, g, pt: (layer, pt[b, g * pages + p], 0, 0, 0, 0))

    def v_spec(p):
        return pl.BlockSpec((1, 1, page * heads, hw), lambda b, g, pt: (layer, pt[b, g * pages + p], 0, 0))

    grid_spec = pltpu.PrefetchScalarGridSpec(
        num_scalar_prefetch=1,
        grid=(nb, n_groups),
        in_specs=[pl.BlockSpec(lam.shape, lambda b, g, pt: (0, 0)), tok, tok, tok]
        + [k_spec(p) for p in range(pages)] + [v_spec(p) for p in range(pages)],
        out_specs=tok,
        scratch_shapes=[pltpu.VMEM((heads, 2 * ts, 1), F32), pltpu.VMEM((heads, 2 * ts, 1), F32),
                        pltpu.VMEM((heads, 2 * ts, hw), F32)],
    )
    return pl.pallas_call(
        kern,
        grid_spec=grid_spec,
        out_shape=jax.ShapeDtypeStruct((n, width), F32),
        compiler_params=_params("parallel", "arbitrary"),
        name="paged_attention",
    )(page_table, lam, q, k_new, v_new, *([cache_k] * pages), *([cache_v] * pages))


def _post_kernel(y_ref, r_ref, k_ref, v_ref, g_ref, o_ref, ga_ref, gb_ref, x_ref, g1_ref, sh2_ref, sc2_ref,
                 lng_ref, lnb_ref, rk_ref, sub_ref, n2_ref, ones_ref, wor_ref, wod_ref, wout_ref, *rest,
                 hs, v_dim, lam_init, n_experts, precise):
    if n_experts:
        router_ref, x_out_ref, h2_ref, gate_ref = rest
    else:
        x_out_ref, h2_ref = rest
    ones = ones_ref[...]
    y = y_ref[...]
    inv_hs = 1.0 / hs
    mu = _dot_exact_rhs(y, ones) * inv_hs
    yc = y - mu
    var = _dot_exact_rhs(yc * yc, ones) * inv_hs
    yn = yc * lax.rsqrt(var + LN_X_EPS) * lng_ref[...] + lnb_ref[...]
    v = v_ref[...]
    bonus = _dot_exact_rhs(r_ref[...] * k_ref[...] * rk_ref[...], ones) * v
    out_a = _mm((yn + bonus) * g_ref[...], wor_ref[...], precise)

    o = o_ref[...]
    parts = []
    for h in range(o.shape[1] // v_dim):
        oh = o[:, h * v_dim:(h + 1) * v_dim]
        parts.append(oh * lax.rsqrt(jnp.mean(oh * oh, axis=-1, keepdims=True) + NORM_EPS))
    on = jnp.concatenate(parts, axis=1) * sub_ref[...] * (1.0 - lam_init)
    out_b = _mm(on, wod_ref[...], precise)

    merged = _sigmoid(ga_ref[...]) * out_a + _sigmoid(gb_ref[...]) * out_b
    x = x_ref[...] + g1_ref[0] * _mm(merged, wout_ref[...], precise)
    x_out_ref[...] = x
    xn = x * lax.rsqrt(jnp.mean(x * x, axis=-1, keepdims=True) + NORM_EPS)
    h2 = (xn * n2_ref[...]) * (1.0 + sc2_ref[0]) + sh2_ref[0]
    h2_ref[...] = h2.astype(h2_ref.dtype)
    if n_experts:
        logits = _dot3(h2, router_ref[...])
        lane = _iota(logits.shape, 1).astype(F32)
        big = float(LANES)
        lg = jnp.where(lane < n_experts, logits, -jnp.inf)
        m1 = jnp.max(lg, axis=-1, keepdims=True)
        i1 = jnp.min(jnp.where(lg == m1, lane, big), axis=-1, keepdims=True)
        lg2 = jnp.where(lane == i1, -jnp.inf, lg)
        m2 = jnp.max(lg2, axis=-1, keepdims=True)
        i2 = jnp.min(jnp.where(lg2 == m2, lane, big), axis=-1, keepdims=True)
        e2 = jnp.exp(m2 - m1)
        den = 1.0 + e2
        gate_ref[...] = jnp.where(lane == i1, 1.0 / den, 0.0) + jnp.where(lane == i2, e2 / den, 0.0)


def _post_call(y, r, k, v, g, o, proj, x, rows, lw, ones_head, gate_cols, lam_init, hs, v_dim):
    n, d = x.shape
    width = y.shape[1]
    tm = _tile(n, 256)
    router = lw.get("router_pad")
    n_experts = lw["n_experts"] if router is not None else 0
    tok = lambda wd: pl.BlockSpec((tm, wd), lambda i: (i, 0))
    row_spec = lambda wd: pl.BlockSpec((1, wd), lambda i: (0, 0))
    full = lambda a: pl.BlockSpec(a.shape, lambda i: (0,) * a.ndim)
    in_specs = [tok(width)] * 6 + [
        pl.BlockSpec((tm, d), lambda i: (i, gate_cols[0])), pl.BlockSpec((tm, d), lambda i: (i, gate_cols[1])),
        tok(d), rows.mod_spec(tm, d, 2, 1), rows.mod_spec(tm, d, 3, 1), rows.mod_spec(tm, d, 4, 1),
        row_spec(width), row_spec(width), row_spec(width), row_spec(width), row_spec(d), full(ones_head),
        full(lw["w_o_rwkv"]), full(lw["w_o_diff"]), full(lw["w_out"]),
    ]
    args = [y, r, k, v, g, o, proj, proj, x, rows.mod3, rows.mod3, rows.mod3,
            lw["ln_x_g"], lw["ln_x_b"], lw["r_k"], lw["subln_g"], lw["norm2_g"], ones_head,
            lw["w_o_rwkv"], lw["w_o_diff"], lw["w_out"]]
    out_specs = [tok(d), tok(d)]
    precise = lw["w_out"].dtype == F32
    out_shape = [jax.ShapeDtypeStruct((n, d), F32), jax.ShapeDtypeStruct((n, d), F32 if precise else BF16)]
    if n_experts:
        in_specs.append(full(router))
        args.append(router)
        out_specs.append(tok(LANES))
        out_shape.append(jax.ShapeDtypeStruct((n, LANES), F32))
    kern = functools.partial(_post_kernel, hs=hs, v_dim=v_dim, lam_init=lam_init, n_experts=n_experts,
                             precise=precise)
    return pl.pallas_call(
        kern,
        grid=(n // tm,),
        in_specs=in_specs,
        out_specs=out_specs,
        out_shape=out_shape,
        compiler_params=_params("parallel"),
        name="merge_post",
    )(*args)


def _ffn_kernel(*refs, moe, precise):
    if moe:
        h_ref, x_ref, g2_ref, gate_ref, w1_ref, w3_ref, w2_ref, o_ref, acc_scr = refs
    else:
        h_ref, x_ref, g2_ref, w1_ref, w3_ref, w2_ref, o_ref, acc_scr = refs
    e = pl.program_id(1)
    f = pl.program_id(2)

    @pl.when((e == 0) & (f == 0))
    def _():
        acc_scr[...] = jnp.zeros(acc_scr.shape, F32)

    h = h_ref[...]
    a = _mm(h, w1_ref[0, 0], precise)
    b = _mm(h, w3_ref[0, 0], precise)
    act = a * _sigmoid(a) * b
    if moe:
        gate = gate_ref[...]
        lane = _iota(gate.shape, 1)
        act = act * jnp.sum(jnp.where(lane == e, gate, 0.0), axis=-1, keepdims=True)
    acc_scr[...] += _mm(act, w2_ref[0, 0], precise)

    @pl.when((e == pl.num_programs(1) - 1) & (f == pl.num_programs(2) - 1))
    def _():
        o_ref[...] = x_ref[...] + g2_ref[0] * acc_scr[...]


def _ffn_call(h2, x, rows, gate, w1, w3, w2, li, tf):
    n, d = x.shape
    _, n_exp, _, ff = w1.shape
    tm = _tile(n, 512 if w1.dtype == BF16 else 256)
    moe = gate is not None
    tok = lambda wd: pl.BlockSpec((tm, wd), lambda i, e, f: (i, 0))
    in_specs = [tok(d), tok(d), rows.mod_spec(tm, d, 5, 3)]
    args = [h2, x, rows.mod3]
    if moe:
        in_specs.append(tok(LANES))
        args.append(gate)
    in_specs += [pl.BlockSpec((1, 1, d, tf), lambda i, e, f: (li, e, 0, f)),
                 pl.BlockSpec((1, 1, d, tf), lambda i, e, f: (li, e, 0, f)),
                 pl.BlockSpec((1, 1, tf, d), lambda i, e, f: (li, e, f, 0))]
    args += [w1, w3, w2]
    return pl.pallas_call(
        functools.partial(_ffn_kernel, moe=moe, precise=w1.dtype == F32),
        grid=(n // tm, n_exp, ff // tf),
        in_specs=in_specs,
        out_specs=tok(d),
        out_shape=jax.ShapeDtypeStruct((n, d), F32),
        scratch_shapes=[pltpu.VMEM((tm, d), F32)],
        compiler_params=_params("parallel", "arbitrary", "arbitrary"),
        name="moe_ffn" if moe else "dense_ffn",
    )(*args)


def _ff_tile(ff):
    for t in (1408, 896, 512, 256, 128):
        if ff % t == 0 and ff // t >= 2:
            return t
    return ff


MOE_ROW_BLOCK = 144


def _moe_routed_kernel(nblk_ref, h_ref, x_ref, g2_ref, gate_ref, tri_ref, w1_ref, w3_ref, w2_ref, o_ref,
                       acc_scr, slotc_scr, slotr_scr, hsel_scr, ysel_scr, gsel_scr, *, rb, n_exp):
    i = pl.program_id(0)
    e = pl.program_id(1)
    f = pl.program_id(2)
    last_f = pl.num_programs(2) - 1
    tm = h_ref.shape[0]
    nb = nblk_ref[i * n_exp + e]
    dot = functools.partial(jnp.dot, preferred_element_type=F32)

    @pl.when((e == 0) & (f == 0))
    def _():
        acc_scr[...] = jnp.zeros(acc_scr.shape, F32)
        routed = gate_ref[...] > 0.0
        ind = jnp.where(routed, 1.0, 0.0).astype(BF16)
        tri = tri_ref[...]
        rank = dot(tri, ind)
        slotc_scr[...] = jnp.where(routed, rank, -1.0)
        eye = (_iota((LANES, LANES), 0) == _iota((LANES, LANES), 1)).astype(BF16)
        ind_t = _dot_nt(eye, ind)
        rank_t = _dot_nt(ind_t.astype(BF16), tri)
        slotr_scr[...] = jnp.where(ind_t > 0.5, rank_t, -1.0)

    @pl.when(f == 0)
    def _():
        slot_row = slotr_scr[pl.ds(e, 1), :]

        def select(b, carry):
            r0 = pl.multiple_of(b * rb, rb)
            want = (r0 + _iota((rb, 1), 0)).astype(F32)
            onehot = jnp.where(slot_row == want, 1.0, 0.0).astype(BF16)
            hsel_scr[pl.ds(r0, rb), :] = dot(onehot, h_ref[...]).astype(BF16)
            g = _dot_exact_lhs(onehot, gate_ref[...])
            ge = jnp.sum(jnp.where(_iota(g.shape, 1) == e, g, 0.0), axis=-1, keepdims=True)
            gsel_scr[pl.ds(r0, rb), :] = jnp.broadcast_to(ge, (rb, LANES))
            ysel_scr[pl.ds(r0, rb), :] = jnp.zeros((rb, ysel_scr.shape[1]), F32)
            return carry

        lax.fori_loop(0, nb, select, 0)

    def expert(b, carry):
        r0 = pl.multiple_of(b * rb, rb)
        hs = hsel_scr[pl.ds(r0, rb), :]
        a = dot(hs, w1_ref[0, 0])
        act = a * _sigmoid(a) * dot(hs, w3_ref[0, 0]) * gsel_scr[pl.ds(r0, rb), 0:1]
        ysel_scr[pl.ds(r0, rb), :] += dot(act.astype(BF16), w2_ref[0, 0])
        return carry

    lax.fori_loop(0, nb, expert, 0)

    @pl.when(f == last_f)
    def _():
        slotc = slotc_scr[...]
        slot_col = jnp.sum(jnp.where(_iota(slotc.shape, 1) == e, slotc, 0.0), axis=-1, keepdims=True)

        def scatter(b, carry):
            r0 = pl.multiple_of(b * rb, rb)
            want = (r0 + _iota((1, rb), 1)).astype(F32)
            onehot_t = jnp.where(slot_col == want, 1.0, 0.0).astype(BF16)
            acc_scr[...] += dot(onehot_t, ysel_scr[pl.ds(r0, rb), :].astype(BF16))
            return carry

        lax.fori_loop(0, nb, scatter, 0)

    @pl.when((e == n_exp - 1) & (f == last_f))
    def _():
        o_ref[...] = x_ref[...] + g2_ref[0] * acc_scr[...]


def _moe_routed_call(h2, x, rows, gate, w1, w3, w2, li, tf):
    n, d = x.shape
    _, n_exp, _, ff = w1.shape
    rb = MOE_ROW_BLOCK
    tm = _tile(n, 1024)
    n_tiles = n // tm
    sel_rows = -(-tm // rb) * rb
    counts = jnp.sum((gate[:, :n_exp] > 0.0).reshape(n_tiles, tm, n_exp).astype(jnp.int32), axis=1)
    nblk = ((counts + rb - 1) // rb).reshape(-1)
    t = jnp.arange(tm)
    tri = (t[None, :] < t[:, None]).astype(BF16)
    tok = lambda wd: pl.BlockSpec((tm, wd), lambda i, e, f, nb: (i, 0))
    mod = rows.mod_spec(tm, d, 5, 3)
    grid_spec = pltpu.PrefetchScalarGridSpec(
        num_scalar_prefetch=1,
        grid=(n_tiles, n_exp, ff // tf),
        in_specs=[tok(d), tok(d), pl.BlockSpec(mod.block_shape, lambda i, e, f, nb: mod.index_map(i)), tok(LANES),
                  pl.BlockSpec((tm, tm), lambda i, e, f, nb: (0, 0)),
                  pl.BlockSpec((1, 1, d, tf), lambda i, e, f, nb: (li, e, 0, f)),
                  pl.BlockSpec((1, 1, d, tf), lambda i, e, f, nb: (li, e, 0, f)),
                  pl.BlockSpec((1, 1, tf, d), lambda i, e, f, nb: (li, e, f, 0))],
        out_specs=tok(d),
        scratch_shapes=[pltpu.VMEM((tm, d), F32), pltpu.VMEM((tm, LANES), F32), pltpu.VMEM((LANES, tm), F32),
                        pltpu.VMEM((sel_rows, d), BF16), pltpu.VMEM((sel_rows, d), F32),
                        pltpu.VMEM((sel_rows, LANES), F32)],
    )
    return pl.pallas_call(
        functools.partial(_moe_routed_kernel, rb=rb, n_exp=n_exp),
        grid_spec=grid_spec,
        out_shape=jax.ShapeDtypeStruct((n, d), F32),
        compiler_params=_params("parallel", "arbitrary", "arbitrary"),
        name="moe_routed",
    )(nblk, h2, x, rows.mod3, gate, tri, w1, w3, w2)


def _block_ones(width, seg):
    i = jnp.arange(width) // seg
    return (i[:, None] == i[None, :]).astype(BF16)


def _rope_tables(pos, qk_dim, n_seg):
    half = qk_dim // 2
    inv_freq = ROPE_THETA ** (-jnp.arange(half, dtype=F32) / half)
    ang = pos.astype(F32)[:, None] * inv_freq[None, :]
    cos, sin = jnp.cos(ang), jnp.sin(ang)
    cos_t = jnp.tile(jnp.concatenate([cos, cos], axis=1), (1, n_seg))
    sin_t = jnp.tile(jnp.concatenate([-sin, sin], axis=1), (1, n_seg))
    return cos_t, sin_t


def kernel(x_prompt, x_sample, c_prompt, c_sample, cache_k, cache_v, state_wkv, state_shift, page_table,
           norm1_g, norm2_g, w_ada, b_ada, w_in, mu_shift, w0, w_up, a0, a_up, g_up, k_k, k_a, r_k,
           ln_x_g, ln_x_b, w_o_rwkv, q_norm_g, k_norm_g, lam, subln_g, w_o_diff, w_out,
           ffn_w1, ffn_w3, ffn_w2, router, moe_w1, moe_w3, moe_w2):
    n_batch, seq, d = x_prompt.shape
    n_dec, dec_seq, _ = x_sample.shape
    depth = w_in.shape[0]
    heads_r, hs = r_k.shape[1], r_k.shape[2]
    width_r = heads_r * hs
    proj_r = mu_shift.shape[-1]
    n_phys, page, heads_d, _, qk_dim = cache_k.shape[1:]
    v_dim = cache_v.shape[-1]
    qk_w = heads_d * 2 * qk_dim
    v_w = heads_d * v_dim
    w_lora, a_lora = w_up.shape[1], a_up.shape[1]
    assert v_dim == 2 * qk_dim == LANES and qk_w == v_w and w_lora + a_lora == LANES
    past_len = page_table.shape[1] * page
    n_p, n_s = n_batch * seq, n_dec * dec_seq

    o_q = proj_r
    o_k, o_v, o_g = o_q + qk_w, o_q + 2 * qk_w, o_q + 2 * qk_w + v_w
    w_in_f = jnp.concatenate([w_in[:, :, o_g:], w_in[:, :, o_q:o_g], w_in[:, :, :o_q]], axis=-1)
    w_in_b = w_in_f.astype(BF16)
    n_gate = w_in.shape[-1] - o_g
    assert n_gate == 2 * d and (n_gate + 3 * qk_w) % proj_r == 0 and n_gate % qk_w == 0
    gate_cols = (0, 1)
    qkv_cols = tuple(n_gate // qk_w + j for j in range(3))
    rwkv_col = (n_gate + 3 * qk_w) // proj_r
    tn = proj_r

    c_all = jnp.concatenate([c_prompt, c_sample], axis=0)
    c_all = jnp.pad(c_all, ((0, (-c_all.shape[0]) % 8), (0, 0)))
    mod = _ada_call(c_all, w_ada, b_ada)

    ones_head = _block_ones(width_r, hs)
    ones_qk = _block_ones(qk_w, qk_dim)
    cos_p, sin_p = _rope_tables(jnp.arange(seq), qk_dim, qk_w // qk_dim)
    cos_s, sin_s = _rope_tables(past_len + jnp.arange(dec_seq), qk_dim, qk_w // qk_dim)
    cos_s, sin_s = jnp.tile(cos_s, (n_dec, 1)), jnp.tile(sin_s, (n_dec, 1))
    cache_kt = jnp.transpose(cache_k, (0, 1, 3, 4, 5, 2))
    cache_v4 = cache_v.reshape(depth, n_phys, page * heads_d, v_dim)

    zpad = lambda m, before, after: jnp.pad(m, ((before, after), (0, 0)))
    bf = lambda a: a.astype(BF16)
    xp = x_prompt.reshape(n_p, d)
    xs = x_sample.reshape(n_s, d)
    shift0 = jnp.zeros((n_batch, 1, proj_r), F32)
    wkv_zero = jnp.zeros((n_batch, heads_r, hs, hs), F32)
    wkv_chunk = _tile(seq, 64)
    dec_chunk = max(16, dec_seq)
    outs = {key: [] for key in ("kp", "vp", "ks", "vs", "wp", "ws", "sp", "ss")}
    dense_f = (ffn_w1[:, None], ffn_w3[:, None], ffn_w2[:, None])
    moe_f = (moe_w1, moe_w3, moe_w2)
    dense_b = tuple(bf(w) for w in dense_f)
    moe_b = tuple(bf(w) for w in moe_f)

    for l in range(depth):
        lam_init = 0.8 - 0.6 * math.exp(-0.3 * l)
        lw = {
            "mu_shift": mu_shift[l][None], "w0": w0[l][None], "a0": a0[l][None],
            "w_up_pad": zpad(w_up[l], 0, a_lora), "a_up_pad": zpad(a_up[l], w_lora, 0), "g_up": g_up[l],
            "k_k": k_k[l][None], "k_a": k_a[l][None], "r_k": r_k[l].reshape(1, width_r),
            "ln_x_g": ln_x_g[l][None], "ln_x_b": ln_x_b[l][None],
            "subln_g": jnp.tile(subln_g[l], heads_d)[None], "norm2_g": norm2_g[l][None],
        }
        qg = jnp.tile(q_norm_g[l].reshape(-1), heads_d)[None]
        kg = jnp.tile(k_norm_g[l].reshape(-1), heads_d)[None]
        i = l // 2
        if l % 2 == 0:
            ffn_f, ffn_b = dense_f, dense_b
        else:
            n_experts = router.shape[-1]
            lw["router_pad"] = jnp.pad(router[i], ((0, 0), (0, LANES - n_experts)))
            lw["n_experts"] = n_experts
            ffn_f, ffn_b = moe_f, moe_b
        tf = _ff_tile(ffn_f[0].shape[-1])
        norm1 = norm1_g[l][None]
        out_w = {"w_o_rwkv": w_o_rwkv[l], "w_o_diff": w_o_diff[l], "w_out": w_out[l]}
        lw_p = dict(lw, **{key: bf(val) for key, val in out_w.items()})
        lw_s = dict(lw, **out_w)

        def run_group(x, rows, init_rows, state0, is_prompt):
            lw_g, w_in_g, ffn = (lw_p, w_in_b[l], ffn_b) if is_prompt else (lw_s, w_in_f[l], ffn_f)
            proj = _proj_call(x, rows, norm1, w_in_g, tn)
            r, ld, k2, v_r, av, bv, g = _rwkv_prep_call(proj, rows, init_rows, lw_g, rwkv_col, ones_head,
                                                        not is_prompt)
            if is_prompt:
                y, state = _wkv_call(r, ld, k2, v_r, av, bv, state0, rows.seq_len, wkv_chunk, False,
                                     n_sub=WKV_CHUNKS_PER_STEP if seq % (wkv_chunk * WKV_CHUNKS_PER_STEP) == 0 else 1)
                k_rows, v_rows, q_bf, k_bf, v_aug = _attn_prep_call(
                    proj, rows, cos_p, sin_p, qg, kg, ones_qk, qkv_cols, BF16, qk_dim)
                o = _flash_call(q_bf, k_bf, v_aug, lam[l], n_batch, seq, heads_d, qk_dim, lam_init)
            else:
                padt = lambda a: jnp.pad(a.reshape(n_dec, dec_seq, width_r),
                                         ((0, 0), (0, dec_chunk - dec_seq), (0, 0))).reshape(-1, width_r)
                y, state = _wkv_call(padt(r), padt(ld), padt(k2), padt(v_r), padt(av), padt(bv), state0,
                                     dec_chunk, dec_chunk, True)
                y = y.reshape(n_dec, dec_chunk, width_r)[:, :dec_seq].reshape(-1, width_r)
                k_rows, v_rows, q_f, _, _ = _attn_prep_call(
                    proj, rows, cos_s, sin_s, qg, kg, ones_qk, qkv_cols, F32, qk_dim)
                o = _paged_call(page_table, lam[l], q_f, k_rows, v_rows, cache_kt, cache_v4, l, heads_d,
                                qk_dim, lam_init, dec_seq)
            post = _post_call(y, r, k2, v_r, g, o, proj, x, rows, lw_g, ones_head, gate_cols, lam_init, hs, v_dim)
            x_new, h2 = post[0], post[1]
            gate = post[2] if len(post) > 2 else None
            ffn_call = _moe_routed_call if (is_prompt and gate is not None) else _ffn_call
            x_out = ffn_call(h2, x_new, rows, gate, *ffn, i, tf)
            shift = proj.reshape(-1, rows.seq_len, proj.shape[-1])[:, -1, rwkv_col * proj_r:]
            return x_out, k_rows, v_rows, state, shift

        rows_p = _Rows(n_p, seq, mod[l, :n_batch][:, None, :])
        rows_s = _Rows(n_s, dec_seq, jnp.repeat(mod[l, n_batch:n_batch + n_dec], dec_seq, axis=0)[None])
        xp, kr, vr, st, sh = run_group(xp, rows_p, shift0, wkv_zero, True)
        outs["kp"].append(kr.reshape(n_batch, seq, heads_d, 2, qk_dim))
        outs["vp"].append(vr.reshape(n_batch, seq, heads_d, v_dim))
        outs["wp"].append(st)
        outs["sp"].append(sh)
        init_s = jnp.repeat(state_shift[l], dec_seq, axis=0)
        xs, kr, vr, st, sh = run_group(xs, rows_s, init_s, state_wkv[l], False)
        outs["ks"].append(kr.reshape(n_dec, dec_seq, heads_d, 2, qk_dim))
        outs["vs"].append(vr.reshape(n_dec, dec_seq, heads_d, v_dim))
        outs["ws"].append(st)
        outs["ss"].append(sh)

    st = lambda key: jnp.stack(outs[key])
    return (xp.reshape(n_batch, seq, d), xs.reshape(n_dec, dec_seq, d), st("kp"), st("vp"), st("ks"), st("vs"),
            st("wp"), st("ws"), st("sp"), st("ss"))
```

```python
import functools
import math

import jax
import jax.numpy as jnp
from jax import lax
from jax.experimental import pallas as pl
from jax.experimental.pallas import tpu as pltpu

F32 = jnp.float32
BF16 = jnp.bfloat16

NORM_EPS = 1e-6
LN_X_EPS = 64e-5
ROPE_THETA = 10000.0
TOP_K = 2
KK_NORM_FLOOR = 1e-12
LANES = 128
VMEM_LIMIT = 56 * 1024 * 1024
FLASH_ROW_PARTS = 4
WKV_CHUNKS_PER_STEP = 4


def _params(*sem):
    return pltpu.CompilerParams(dimension_semantics=sem, vmem_limit_bytes=VMEM_LIMIT)


def _tile(n, pref):
    if n <= pref:
        return n
    t = pref
    while n % t:
        t //= 2
    return t


def _bdot(a, b):
    return jnp.dot(a.astype(BF16), b.astype(BF16), preferred_element_type=F32)


def _mm(a, b, precise):
    return _dot3(a.astype(F32), b.astype(F32)) if precise else _bdot(a, b)


def _dot_nt(a, b):
    return lax.dot_general(a, b, (((1,), (1,)), ((), ())), preferred_element_type=F32)


def _split2(x):
    hi = x.astype(BF16)
    lo = (x - hi.astype(F32)).astype(BF16)
    return hi, lo


def _split3(x):
    hi = x.astype(BF16)
    r1 = x - hi.astype(F32)
    mid = r1.astype(BF16)
    lo = (r1 - mid.astype(F32)).astype(BF16)
    return hi, mid, lo


def _dot3(a, b):
    ah, al = _split2(a)
    bh, bl = _split2(b)
    d = functools.partial(jnp.dot, preferred_element_type=F32)
    return d(ah, bh) + d(ah, bl) + d(al, bh)


def _dot3_nt(a, b):
    ah, al = _split2(a)
    bh, bl = _split2(b)
    return _dot_nt(ah, bh) + _dot_nt(ah, bl) + _dot_nt(al, bh)


def _dot_exact_rhs(x, m_bf):
    hi, mid, lo = _split3(x)
    d = functools.partial(jnp.dot, preferred_element_type=F32)
    return d(hi, m_bf) + d(mid, m_bf) + d(lo, m_bf)


def _dot_exact_lhs(m_bf, x):
    hi, mid, lo = _split3(x)
    d = functools.partial(jnp.dot, preferred_element_type=F32)
    return d(m_bf, hi) + d(m_bf, mid) + d(m_bf, lo)


def _sigmoid(x):
    return 1.0 / (1.0 + jnp.exp(-x))


def _softplus(x):
    return jnp.maximum(x, 0.0) + jnp.log(1.0 + jnp.exp(-jnp.abs(x)))


def _iota(shape, dim):
    return lax.broadcasted_iota(jnp.int32, shape, dim)


def _ada_kernel(c_ref, w_ref, b_ref, o_ref):
    o_ref[0] = _dot3(c_ref[...], w_ref[0]) + b_ref[0]


def _ada_call(c_all, w_ada, b_ada):
    n_layers, d, d6 = w_ada.shape
    m = c_all.shape[0]
    tn = _tile(d6, 1536)
    return pl.pallas_call(
        _ada_kernel,
        grid=(n_layers, d6 // tn),
        in_specs=[
            pl.BlockSpec((m, d), lambda l, j: (0, 0)),
            pl.BlockSpec((1, d, tn), lambda l, j: (l, 0, j)),
            pl.BlockSpec((1, 1, tn), lambda l, j: (l, 0, j)),
        ],
        out_specs=pl.BlockSpec((1, m, tn), lambda l, j: (l, 0, j)),
        out_shape=jax.ShapeDtypeStruct((n_layers, m, d6), F32),
        compiler_params=_params("parallel", "parallel"),
        name="adaln_mod",
    )(c_all, w_ada, b_ada.reshape(n_layers, 1, d6))


class _Rows:
    def __init__(self, n, seq_len, mod3):
        self.n, self.seq_len, self.mod3 = n, seq_len, mod3
        self.per_row = mod3.shape[1] != 1

    def mod_spec(self, tm, d, col, n_grid):
        r = self.mod3.shape[1]
        if self.per_row:
            assert r == tm == self.n
            idx = lambda i, *_: (0, 0, col)
        else:
            assert self.seq_len % tm == 0
            per = self.seq_len // tm
            idx = lambda i, *_: (i // per, 0, col)
        return pl.BlockSpec((1, r, d), idx)


def _proj_kernel(x_ref, sh_ref, sc_ref, g_ref, w_ref, o_ref, h_scr, *, precise):
    @pl.when(pl.program_id(1) == 0)
    def _():
        x = x_ref[...]
        y = x * lax.rsqrt(jnp.mean(x * x, axis=-1, keepdims=True) + NORM_EPS)
        h = (y * g_ref[...]) * (1.0 + sc_ref[0]) + sh_ref[0]
        h_scr[...] = h.astype(h_scr.dtype)

    o_ref[...] = _mm(h_scr[...], w_ref[...], precise)


def _proj_call(x, rows, norm_g, w, tn):
    n, d = x.shape
    pw = w.shape[1]
    tm = _tile(n, 512)
    precise = w.dtype == F32
    return pl.pallas_call(
        functools.partial(_proj_kernel, precise=precise),
        grid=(n // tm, pw // tn),
        in_specs=[
            pl.BlockSpec((tm, d), lambda i, j: (i, 0)),
            rows.mod_spec(tm, d, 0, 2),
            rows.mod_spec(tm, d, 1, 2),
            pl.BlockSpec((1, d), lambda i, j: (0, 0)),
            pl.BlockSpec((d, tn), lambda i, j: (0, j)),
        ],
        out_specs=pl.BlockSpec((tm, tn), lambda i, j: (i, j)),
        out_shape=jax.ShapeDtypeStruct((n, pw), F32),
        scratch_shapes=[pltpu.VMEM((tm, d), w.dtype)],
        compiler_params=_params("parallel", "arbitrary"),
        name="proj_in",
    )(x, rows.mod3, rows.mod3, norm_g, w)


def _rwkv_prep_kernel(cols_ref, prev8_ref, init_ref, mu_ref, w0_ref, wup_ref, a0_ref, aup_ref,
                      gup_ref, kk_ref, ka_ref, ones_ref,
                      r_ref, ld_ref, k_ref, v_ref, av_ref, bv_ref, g_ref, *, seq_len, tm, width, precise):
    cols = cols_ref[...]
    rolled = pltpu.roll(cols, 1, 0)
    row = _iota((tm, 1), 0)
    if seq_len >= tm:
        first_tile = (pl.program_id(0) % (seq_len // tm)) == 0
        boundary = jnp.where(first_tile, init_ref[0], prev8_ref[7:8, :])
        prev = jnp.where(row == 0, boundary, rolled)
    else:
        prev = jnp.where(row % seq_len == 0, init_ref[...], rolled)
    xs = cols + (prev - cols) * mu_ref[...]
    w = width
    r = xs[:, 0:w]
    k = xs[:, w:2 * w]
    v = xs[:, 2 * w:3 * w]
    lora_wa = xs[:, 3 * w:3 * w + LANES]
    gd = xs[:, 3 * w + LANES:]
    w_log = -_softplus(-(w0_ref[...] + _mm(jnp.tanh(lora_wa), wup_ref[...], precise))) - 0.5
    a = _sigmoid(a0_ref[...] + _mm(lora_wa, aup_ref[...], precise))
    g = _mm(_sigmoid(gd), gup_ref[...], precise)
    kk = k * kk_ref[...]
    ss = _dot_exact_rhs(kk * kk, ones_ref[...])
    kk = kk / jnp.maximum(jnp.sqrt(ss), KK_NORM_FLOOR)
    r_ref[...] = r
    ld_ref[...] = -jnp.exp(w_log)
    k_ref[...] = k * (1.0 + (a - 1.0) * ka_ref[...])
    v_ref[...] = v
    av_ref[...] = -kk
    bv_ref[...] = kk * a
    g_ref[...] = g


def _rwkv_prep_call(proj, rows, init_rows, lw, col_block, ones_head, precise):
    n = proj.shape[0]
    width = lw["w0"].shape[-1]
    pw = lw["mu_shift"].shape[-1]
    tm = _tile(n, 256) if rows.seq_len >= 256 else n
    assert pw == 3 * width + LANES + lw["g_up"].shape[0]
    if rows.seq_len >= tm:
        init_spec = pl.BlockSpec((1, 1, pw), lambda i: ((i * tm) // rows.seq_len, 0, 0))
    else:
        init_spec = pl.BlockSpec((tm, pw), lambda i: (0, 0))
    row_spec = lambda wd: pl.BlockSpec((1, wd), lambda i: (0, 0))
    full = lambda a: pl.BlockSpec(a.shape, lambda i: (0,) * a.ndim)
    out = jax.ShapeDtypeStruct((n, width), F32)
    kern = functools.partial(_rwkv_prep_kernel, seq_len=rows.seq_len, tm=tm, width=width, precise=precise)
    return pl.pallas_call(
        kern,
        grid=(n // tm,),
        in_specs=[
            pl.BlockSpec((tm, pw), lambda i: (i, col_block)),
            pl.BlockSpec((8, pw), lambda i: (jnp.maximum(i * (tm // 8) - 1, 0), col_block)),
            init_spec,
            row_spec(pw), row_spec(width), full(lw["w_up_pad"]), row_spec(width), full(lw["a_up_pad"]),
            full(lw["g_up"]), row_spec(width), row_spec(width), full(ones_head),
        ],
        out_specs=[pl.BlockSpec((tm, width), lambda i: (i, 0))] * 7,
        out_shape=[out] * 7,
        compiler_params=_params("parallel"),
        name="rwkv_prep",
    )(proj, proj, init_rows, lw["mu_shift"], lw["w0"], lw["w_up_pad"], lw["a0"], lw["a_up_pad"],
      lw["g_up"], lw["k_k"], lw["k_a"], ones_head)


def _wkv_kernel(r_ref, ld_ref, k_ref, v_ref, a_ref, b_ref, s0_ref, y_ref, s_ref, *, chunk, heads, hs, precise):
    c = chunk
    if precise:
        mm, mm_nt = _dot3, _dot3_nt
    else:
        mm = _bdot
        mm_nt = lambda a, b: _dot_nt(a.astype(BF16), b.astype(BF16))

    @pl.when(pl.program_id(1) == 0)
    def _():
        s_ref[...] = s0_ref[...]

    rr = _iota((c, c), 0)
    cc = _iota((c, c), 1)
    strict = rr > cc
    incl = rr >= cc
    tri = incl.astype(BF16)
    eye_c = (rr == cc).astype(F32)
    eye_v = (_iota((hs, hs), 0) == _iota((hs, hs), 1)).astype(BF16)

    n_double = int(math.log2(c)) - 1
    hr = range(heads)
    sls = [slice(h * hs, (h + 1) * hs) for h in hr]
    n_sub = r_ref.shape[0] // c
    items = [(ci, h) for ci in range(n_sub) for h in hr]

    pre = []
    for ci in range(n_sub):
        rows = slice(ci * c, (ci + 1) * c)
        ld = ld_ref[rows, :]
        cum = _dot_exact_lhs(tri, ld)
        tot = cum[c - 1:c, :]
        e_neg = jnp.exp(-cum)
        e_dec = jnp.exp(tot - cum)
        b_all, k_all = b_ref[rows, :], k_ref[rows, :]
        pre.append(dict(a_t=a_ref[rows, :] * jnp.exp(cum - ld), r_t=r_ref[rows, :] * jnp.exp(cum),
                        b_t=b_all * e_neg, k_t=k_all * e_neg, b_d=b_all * e_dec, k_d=k_all * e_dec,
                        e_tot=jnp.exp(tot), v=v_ref[rows, :]))
    get = lambda name, ci, h: pre[ci][name][:, sls[h]]

    pm = [mm_nt(jnp.concatenate([get("a_t", ci, h), get("r_t", ci, h)], axis=0),
                jnp.concatenate([get("b_t", ci, h), get("k_t", ci, h)], axis=0)) for ci, h in items]
    l_ab = [jnp.where(strict, p[:c, :c], 0.0) for p in pm]
    l_ak = [jnp.where(strict, p[:c, c:], 0.0) for p in pm]
    m_rb = [jnp.where(incl, p[c:, :c], 0.0) for p in pm]
    m_rk = [jnp.where(incl, p[c:, c:], 0.0) for p in pm]
    t_inv = [eye_c + l for l in l_ab]
    l_pow = l_ab
    for _ in range(n_double):
        l_pow = [mm(l, l) for l in l_pow]
        t_inv = [t + mm(t, l) for t, l in zip(t_inv, l_pow)]
    vh = [get("v", ci, h) for ci, h in items]
    z = [mm(l_ak[j], vh[j]) for j in range(len(items))]
    au = [mm(t_inv[j], jnp.concatenate([get("a_t", ci, h), z[j]], axis=1)) for j, (ci, h) in enumerate(items)]
    state = [s_ref[0, h] for h in hr]
    for ci in range(n_sub):
        rows = slice(ci * c, (ci + 1) * c)
        for h in hr:
            j = ci * heads + h
            s0 = state[h]
            u = mm_nt(au[j][:, :hs], s0) + au[j][:, hs:]
            y = mm_nt(get("r_t", ci, h), s0) + mm(m_rb[j], u) + mm(m_rk[j], vh[j])
            uv = jnp.concatenate([u, vh[j]], axis=0)
            if precise:
                uh, um, ul = _split3(uv)
                uv_t = _dot_nt(eye_v, uh) + _dot_nt(eye_v, um) + _dot_nt(eye_v, ul)
            else:
                uv_t = _dot_nt(eye_v, uv.astype(BF16))
            bkd = jnp.concatenate([get("b_d", ci, h), get("k_d", ci, h)], axis=0)
            state[h] = s0 * pre[ci]["e_tot"][:, sls[h]] + mm(uv_t, bkd)
            y_ref[rows, sls[h]] = y
    for h in hr:
        s_ref[0, h] = state[h]


def _wkv_call(r, ld, k, v, av, bv, state0, seq_len, chunk, precise, n_sub=1):
    n, width = r.shape
    nb, heads, hs, _ = state0.shape
    n_chunks = seq_len // (chunk * n_sub)
    tok = pl.BlockSpec((chunk * n_sub, width), lambda b, c: (b * n_chunks + c, 0))
    st = pl.BlockSpec((1, heads, hs, hs), lambda b, c: (b, 0, 0, 0))
    kern = functools.partial(_wkv_kernel, chunk=chunk, heads=heads, hs=hs, precise=precise)
    return pl.pallas_call(
        kern,
        grid=(nb, n_chunks),
        in_specs=[tok] * 6 + [st],
        out_specs=[tok, st],
        out_shape=[jax.ShapeDtypeStruct((n, width), F32), jax.ShapeDtypeStruct(state0.shape, F32)],
        compiler_params=_params("parallel", "arbitrary"),
        name="wkv_scan",
    )(r, ld, k, v, av, bv, state0)


def _attn_prep_kernel(q_ref, k_ref, v_ref, cos_ref, sin_ref, qg_ref, kg_ref, ones_ref,
                      kr_ref, vr_ref, qo_ref, kb_ref, vb_ref, *, qk_dim, scale):
    ones = ones_ref[...]
    cos = cos_ref[...]
    sin = sin_ref[...]
    lane = _iota(cos.shape, 1)
    first_half = (lane % qk_dim) < (qk_dim // 2)
    width = cos.shape[1]
    half = qk_dim // 2

    def norm_rope(x, g):
        ms = _dot_exact_rhs(x * x, ones) * (1.0 / qk_dim)
        y = x * lax.rsqrt(ms + NORM_EPS) * g
        swapped = jnp.where(first_half, pltpu.roll(y, width - half, 1), pltpu.roll(y, half, 1))
        return y * cos + swapped * sin

    q = norm_rope(q_ref[...], qg_ref[...])
    k = norm_rope(k_ref[...], kg_ref[...])
    v = v_ref[...]
    kr_ref[...] = k
    vr_ref[...] = v
    qo_ref[...] = (q * scale).astype(qo_ref.dtype)
    kb_ref[...] = k.astype(BF16)
    vb = v.astype(BF16)
    hw = 2 * qk_dim
    one = jnp.ones((vb.shape[0], hw), BF16)
    vb_ref[...] = jnp.concatenate(
        [blk for h in range(width // hw) for blk in (vb[:, h * hw:(h + 1) * hw], one)], axis=1)


def _attn_prep_call(proj, rows, cos, sin, qg, kg, ones_qk, col_blocks, q_dtype, qk_dim):
    n = proj.shape[0]
    width = qg.shape[-1]
    tm = _tile(n, 512) if rows.seq_len >= 512 else n
    per = max(rows.seq_len // tm, 1)
    tab = pl.BlockSpec((tm, width), lambda i: (i % per, 0))
    col = lambda cb: pl.BlockSpec((tm, width), lambda i: (i, cb))
    row_spec = pl.BlockSpec((1, width), lambda i: (0, 0))
    out_spec = pl.BlockSpec((tm, width), lambda i: (i, 0))
    kern = functools.partial(_attn_prep_kernel, qk_dim=qk_dim, scale=qk_dim ** -0.5)
    sds = lambda dt: jax.ShapeDtypeStruct((n, width), dt)
    return pl.pallas_call(
        kern,
        grid=(n // tm,),
        in_specs=[col(col_blocks[0]), col(col_blocks[1]), col(col_blocks[2]), tab, tab, row_spec, row_spec,
                  pl.BlockSpec(ones_qk.shape, lambda i: (0, 0))],
        out_specs=[out_spec] * 4 + [pl.BlockSpec((tm, 2 * width), lambda i: (i, 0))],
        out_shape=[sds(F32), sds(F32), sds(q_dtype), sds(BF16), jax.ShapeDtypeStruct((n, 2 * width), BF16)],
        compiler_params=_params("parallel"),
        name="attn_prep",
    )(proj, proj, proj, cos, sin, qg, kg, ones_qk)


def _lambda(lam_ref, lam_init):
    lp = lam_ref[...]
    s1 = jnp.sum(lp[0:1] * lp[1:2], axis=-1, keepdims=True)
    s2 = jnp.sum(lp[2:3] * lp[3:4], axis=-1, keepdims=True)
    return jnp.exp(s1) - jnp.exp(s2) + lam_init


def _stack_maps(qh, qk_dim):
    lane = _iota(qh.shape, 1)
    zero = jnp.zeros_like(qh)
    return jnp.concatenate([jnp.where(lane < qk_dim, qh, zero), jnp.where(lane >= qk_dim, qh, zero)], axis=0)


def _flash_kernel(lam_ref, q_ref, k_ref, v_ref, o_ref, qs_scr, m_scr, acc_scr, *, tq, tk, qk_dim, lam_init,
                  n_parts):
    i = pl.program_id(2)
    rows = 2 * tq
    hw = 2 * qk_dim
    n_blk = tk // hw
    qs_scr[...] = _stack_maps(q_ref[...], qk_dim)
    m_scr[...] = jnp.full(m_scr.shape, -jnp.inf, F32)
    acc_scr[...] = jnp.zeros(acc_scr.shape, F32)
    n_full = (i * tq) // tk

    part = rows // n_parts

    def step(j, masked):
        start = pl.multiple_of(j * tk, tk)
        kc = k_ref[pl.ds(start, tk), :]
        vc = v_ref[pl.ds(start, tk), :]
        for r0 in range(0, rows, part):
            rs = slice(r0, r0 + part)
            s = _dot_nt(qs_scr[rs, :], kc)
            if masked:
                qpos = i * tq + (r0 % tq) + _iota((part, tk), 0)
                kpos = start + _iota((part, tk), 1)
                s = jnp.where(kpos <= qpos, s, -jnp.inf)
            blocks = [s[:, c * hw:(c + 1) * hw] for c in range(n_blk)]
            blk_max = blocks[0]
            for blk in blocks[1:]:
                blk_max = jnp.maximum(blk_max, blk)
            m_prev = m_scr[rs, :]
            m_new = jnp.maximum(m_prev, jnp.max(blk_max, axis=-1, keepdims=True))
            alpha = jnp.exp(m_prev - m_new)
            p = jnp.concatenate([jnp.exp(blk - m_new).astype(BF16) for blk in blocks], axis=1)
            pv = jnp.dot(p, vc, preferred_element_type=F32)
            acc_scr[rs, :] = jnp.concatenate([alpha, alpha], axis=1) * acc_scr[rs, :] + pv
            m_scr[rs, :] = m_new

    def body(j, carry):
        step(j, False)
        return carry

    lax.fori_loop(0, n_full, body, 0)
    step(n_full, True)
    acc = acc_scr[...]
    o = acc[:, :hw] / acc[:, hw:]
    lam = _lambda(lam_ref, lam_init)
    o_ref[...] = o[:tq] - lam * o[tq:]


def _flash_call(q_bf, k_bf, v_aug, lam, n_batch, seq_len, heads, qk_dim, lam_init):
    n, width = q_bf.shape
    hw = width // heads
    tq = _tile(seq_len, 512)
    tk = _tile(seq_len, 1024)
    nq = seq_len // tq
    kern = functools.partial(_flash_kernel, tq=tq, tk=tk, qk_dim=qk_dim, lam_init=lam_init,
                             n_parts=FLASH_ROW_PARTS)
    qspec = pl.BlockSpec((tq, hw), lambda b, h, i: (b * nq + i, h))
    return pl.pallas_call(
        kern,
        grid=(n_batch, heads, nq),
        in_specs=[pl.BlockSpec(lam.shape, lambda b, h, i: (0, 0)), qspec,
                  pl.BlockSpec((seq_len, hw), lambda b, h, i: (b, h)),
                  pl.BlockSpec((seq_len, 2 * hw), lambda b, h, i: (b, h))],
        out_specs=qspec,
        out_shape=jax.ShapeDtypeStruct((n, width), F32),
        scratch_shapes=[pltpu.VMEM((2 * tq, hw), BF16), pltpu.VMEM((2 * tq, hw), F32),
                        pltpu.VMEM((2 * tq, 2 * hw), F32)],
        compiler_params=_params("parallel", "parallel", "arbitrary"),
        name="prompt_attention",
    )(lam, q_bf, k_bf, v_aug)


def _paged_kernel(pt_ref, lam_ref, q_ref, kn_ref, vn_ref, *refs, pages, heads, qk_dim, lam_init, n_groups):
    del pt_ref
    k_refs = refs[:pages]
    v_refs = refs[pages:2 * pages]
    o_ref = refs[2 * pages]
    m_scr, l_scr, acc_scr = refs[2 * pages + 1:]
    g = pl.program_id(1)
    hw = 2 * qk_dim
    ts = q_ref.shape[0]

    @pl.when(g == 0)
    def _():
        m_scr[...] = jnp.full(m_scr.shape, -jnp.inf, F32)
        l_scr[...] = jnp.zeros(l_scr.shape, F32)
        acc_scr[...] = jnp.zeros(acc_scr.shape, F32)

    q = q_ref[...]

    def update(h, s, pv_fn):
        m_prev = m_scr[h]
        m_new = jnp.maximum(m_prev, jnp.max(s, axis=-1, keepdims=True))
        alpha = jnp.exp(m_prev - m_new)
        p = jnp.exp(s - m_new)
        l_scr[h] = alpha * l_scr[h] + jnp.sum(p, axis=-1, keepdims=True)
        acc_scr[h] = alpha * acc_scr[h] + pv_fn(p)
        m_scr[h] = m_new

    def scores(qs, keys):
        qh, ql = _split2(qs)
        kh, kl = _split2(keys)
        both = _dot_nt(jnp.concatenate([qh, ql], axis=0), kh)
        return both[:2 * ts] + both[2 * ts:] + _dot_nt(qh, kl)

    def weighted(p, vals):
        ph, plo = _split2(p)
        vh, vl = _split2(vals)
        both = jnp.dot(jnp.concatenate([ph, plo], axis=0), vh, preferred_element_type=F32)
        return both[:2 * ts] + both[2 * ts:] + jnp.dot(ph, vl, preferred_element_type=F32)

    def scores_t(qs, keys_t):
        qh, ql = _split2(qs)
        kh, kl = _split2(keys_t)
        both = jnp.dot(jnp.concatenate([qh, ql], axis=0), kh, preferred_element_type=F32)
        return both[:2 * ts] + both[2 * ts:] + jnp.dot(qh, kl, preferred_element_type=F32)

    page = k_refs[0].shape[-1]
    hr = range(heads)
    qs = [_stack_maps(q[:, h * hw:(h + 1) * hw], qk_dim) for h in hr]
    keys_t = [jnp.concatenate([k_refs[p][0, 0, h].reshape(hw, page) for p in range(pages)], axis=1) for h in hr]
    s = [scores_t(qs[h], keys_t[h]) for h in hr]
    m_prev = [m_scr[h] for h in hr]
    m_new = [jnp.maximum(m_prev[h], jnp.max(s[h], axis=-1, keepdims=True)) for h in hr]
    alpha = [jnp.exp(m_prev[h] - m_new[h]) for h in hr]
    prob = [jnp.exp(s[h] - m_new[h]) for h in hr]
    vals = [jnp.concatenate([v_refs[p][0, 0, pl.ds(h, page, stride=heads), :] for p in range(pages)], axis=0)
            for h in hr]
    pv = [weighted(prob[h], vals[h]) for h in hr]
    for h in hr:
        l_scr[h] = alpha[h] * l_scr[h] + jnp.sum(prob[h], axis=-1, keepdims=True)
        acc_scr[h] = alpha[h] * acc_scr[h] + pv[h]
        m_scr[h] = m_new[h]

    @pl.when(g == n_groups - 1)
    def _():
        lam = _lambda(lam_ref, lam_init)
        pad = jnp.zeros((LANES - ts, kn_ref.shape[1]), F32)
        kn = jnp.concatenate([kn_ref[...], pad], axis=0)
        vn = jnp.concatenate([vn_ref[...], pad], axis=0)
        qpos = _iota((2 * ts, LANES), 0) % ts
        kpos = _iota((2 * ts, LANES), 1)
        keep = kpos <= qpos
        for h in range(heads):
            sl = slice(h * hw, (h + 1) * hw)
            qs = _stack_maps(q[:, sl], qk_dim)
            s = jnp.where(keep, scores(qs, kn[:, sl]), -jnp.inf)
            update(h, s, lambda p, sl=sl: weighted(p, vn[:, sl]))
            o = acc_scr[h] / l_scr[h]
            o_ref[:, sl] = o[:ts] - lam * o[ts:]


def _paged_call(page_table, lam, q, k_new, v_new, cache_k, cache_v, layer, heads, qk_dim, lam_init, ts):
    n, width = q.shape
    nb, n_pages = page_table.shape
    pages = _tile(n_pages, 16)
    n_groups = n_pages // pages
    page = cache_k.shape[-1]
    hw = 2 * qk_dim
    kern = functools.partial(_paged_kernel, pages=pages, heads=heads, qk_dim=qk_dim, lam_init=lam_init,
                             n_groups=n_groups)
    tok = pl.BlockSpec((ts, width), lambda b, g, pt: (b, 0))

    def k_spec(p):
        return pl.BlockSpec((1, 1, heads, 2, qk_dim, page),
                            lambda b, g, pt: (layer, pt[b, g * pages + p], 0, 0, 0, 0))

    def v_spec(p):
        return pl.BlockSpec((1, 1, page * heads, hw), lambda b, g, pt: (layer, pt[b, g * pages + p], 0, 0))

    grid_spec = pltpu.PrefetchScalarGridSpec(
        num_scalar_prefetch=1,
        grid=(nb, n_groups),
        in_specs=[pl.BlockSpec(lam.shape, lambda b, g, pt: (0, 0)), tok, tok, tok]
        + [k_spec(p) for p in range(pages)] + [v_spec(p) for p in range(pages)],
        out_specs=tok,
        scratch_shapes=[pltpu.VMEM((heads, 2 * ts, 1), F32), pltpu.VMEM((heads, 2 * ts, 1), F32),
                        pltpu.VMEM((heads, 2 * ts, hw), F32)],
    )
    return pl.pallas_call(
        kern,
        grid_spec=grid_spec,
        out_shape=jax.ShapeDtypeStruct((n, width), F32),
        compiler_params=_params("parallel", "arbitrary"),
        name="paged_attention",
    )(page_table, lam, q, k_new, v_new, *([cache_k] * pages), *([cache_v] * pages))


def _post_kernel(y_ref, r_ref, k_ref, v_ref, g_ref, o_ref, ga_ref, gb_ref, x_ref, g1_ref, sh2_ref, sc2_ref,
                 lng_ref, lnb_ref, rk_ref, sub_ref, n2_ref, ones_ref, wor_ref, wod_ref, wout_ref, *rest,
                 hs, v_dim, lam_init, n_experts, precise):
    if n_experts:
        router_ref, x_out_ref, h2_ref, gate_ref = rest
    else:
        x_out_ref, h2_ref = rest
    ones = ones_ref[...]
    y = y_ref[...]
    inv_hs = 1.0 / hs
    mu = _dot_exact_rhs(y, ones) * inv_hs
    yc = y - mu
    var = _dot_exact_rhs(yc * yc, ones) * inv_hs
    yn = yc * lax.rsqrt(var + LN_X_EPS) * lng_ref[...] + lnb_ref[...]
    v = v_ref[...]
    bonus = _dot_exact_rhs(r_ref[...] * k_ref[...] * rk_ref[...], ones) * v
    out_a = _mm((yn + bonus) * g_ref[...], wor_ref[...], precise)

    o = o_ref[...]
    parts = []
    for h in range(o.shape[1] // v_dim):
        oh = o[:, h * v_dim:(h + 1) * v_dim]
        parts.append(oh * lax.rsqrt(jnp.mean(oh * oh, axis=-1, keepdims=True) + NORM_EPS))
    on = jnp.concatenate(parts, axis=1) * sub_ref[...] * (1.0 - lam_init)
    out_b = _mm(on, wod_ref[...], precise)

    merged = _sigmoid(ga_ref[...]) * out_a + _sigmoid(gb_ref[...]) * out_b
    x = x_ref[...] + g1_ref[0] * _mm(merged, wout_ref[...], precise)
    x_out_ref[...] = x
    xn = x * lax.rsqrt(jnp.mean(x * x, axis=-1, keepdims=True) + NORM_EPS)
    h2 = (xn * n2_ref[...]) * (1.0 + sc2_ref[0]) + sh2_ref[0]
    h2_ref[...] = h2.astype(h2_ref.dtype)
    if n_experts:
        logits = _dot3(h2, router_ref[...])
        lane = _iota(logits.shape, 1).astype(F32)
        big = float(LANES)
        lg = jnp.where(lane < n_experts, logits, -jnp.inf)
        m1 = jnp.max(lg, axis=-1, keepdims=True)
        i1 = jnp.min(jnp.where(lg == m1, lane, big), axis=-1, keepdims=True)
        lg2 = jnp.where(lane == i1, -jnp.inf, lg)
        m2 = jnp.max(lg2, axis=-1, keepdims=True)
        i2 = jnp.min(jnp.where(lg2 == m2, lane, big), axis=-1, keepdims=True)
        e2 = jnp.exp(m2 - m1)
        den = 1.0 + e2
        gate_ref[...] = jnp.where(lane == i1, 1.0 / den, 0.0) + jnp.where(lane == i2, e2 / den, 0.0)


def _post_call(y, r, k, v, g, o, proj, x, rows, lw, ones_head, gate_cols, lam_init, hs, v_dim):
    n, d = x.shape
    width = y.shape[1]
    tm = _tile(n, 256)
    router = lw.get("router_pad")
    n_experts = lw["n_experts"] if router is not None else 0
    tok = lambda wd: pl.BlockSpec((tm, wd), lambda i: (i, 0))
    row_spec = lambda wd: pl.BlockSpec((1, wd), lambda i: (0, 0))
    full = lambda a: pl.BlockSpec(a.shape, lambda i: (0,) * a.ndim)
    in_specs = [tok(width)] * 6 + [
        pl.BlockSpec((tm, d), lambda i: (i, gate_cols[0])), pl.BlockSpec((tm, d), lambda i: (i, gate_cols[1])),
        tok(d), rows.mod_spec(tm, d, 2, 1), rows.mod_spec(tm, d, 3, 1), rows.mod_spec(tm, d, 4, 1),
        row_spec(width), row_spec(width), row_spec(width), row_spec(width), row_spec(d), full(ones_head),
        full(lw["w_o_rwkv"]), full(lw["w_o_diff"]), full(lw["w_out"]),
    ]
    args = [y, r, k, v, g, o, proj, proj, x, rows.mod3, rows.mod3, rows.mod3,
            lw["ln_x_g"], lw["ln_x_b"], lw["r_k"], lw["subln_g"], lw["norm2_g"], ones_head,
            lw["w_o_rwkv"], lw["w_o_diff"], lw["w_out"]]
    out_specs = [tok(d), tok(d)]
    precise = lw["w_out"].dtype == F32
    out_shape = [jax.ShapeDtypeStruct((n, d), F32), jax.ShapeDtypeStruct((n, d), F32 if precise else BF16)]
    if n_experts:
        in_specs.append(full(router))
        args.append(router)
        out_specs.append(tok(LANES))
        out_shape.append(jax.ShapeDtypeStruct((n, LANES), F32))
    kern = functools.partial(_post_kernel, hs=hs, v_dim=v_dim, lam_init=lam_init, n_experts=n_experts,
                             precise=precise)
    return pl.pallas_call(
        kern,
        grid=(n // tm,),
        in_specs=in_specs,
        out_specs=out_specs,
        out_shape=out_shape,
        compiler_params=_params("parallel"),
        name="merge_post",
    )(*args)


def _ffn_kernel(*refs, moe, precise):
    if moe:
        h_ref, x_ref, g2_ref, gate_ref, w1_ref, w3_ref, w2_ref, o_ref, acc_scr = refs
    else:
        h_ref, x_ref, g2_ref, w1_ref, w3_ref, w2_ref, o_ref, acc_scr = refs
    e = pl.program_id(1)
    f = pl.program_id(2)

    @pl.when((e == 0) & (f == 0))
    def _():
        acc_scr[...] = jnp.zeros(acc_scr.shape, F32)

    h = h_ref[...]
    a = _mm(h, w1_ref[0, 0], precise)
    b = _mm(h, w3_ref[0, 0], precise)
    act = a * _sigmoid(a) * b
    if moe:
        gate = gate_ref[...]
        lane = _iota(gate.shape, 1)
        act = act * jnp.sum(jnp.where(lane == e, gate, 0.0), axis=-1, keepdims=True)
    acc_scr[...] += _mm(act, w2_ref[0, 0], precise)

    @pl.when((e == pl.num_programs(1) - 1) & (f == pl.num_programs(2) - 1))
    def _():
        o_ref[...] = x_ref[...] + g2_ref[0] * acc_scr[...]


def _ffn_call(h2, x, rows, gate, w1, w3, w2, li, tf):
    n, d = x.shape
    _, n_exp, _, ff = w1.shape
    tm = _tile(n, 512 if w1.dtype == BF16 else 256)
    moe = gate is not None
    tok = lambda wd: pl.BlockSpec((tm, wd), lambda i, e, f: (i, 0))
    in_specs = [tok(d), tok(d), rows.mod_spec(tm, d, 5, 3)]
    args = [h2, x, rows.mod3]
    if moe:
        in_specs.append(tok(LANES))
        args.append(gate)
    in_specs += [pl.BlockSpec((1, 1, d, tf), lambda i, e, f: (li, e, 0, f)),
                 pl.BlockSpec((1, 1, d, tf), lambda i, e, f: (li, e, 0, f)),
                 pl.BlockSpec((1, 1, tf, d), lambda i, e, f: (li, e, f, 0))]
    args += [w1, w3, w2]
    return pl.pallas_call(
        functools.partial(_ffn_kernel, moe=moe, precise=w1.dtype == F32),
        grid=(n // tm, n_exp, ff // tf),
        in_specs=in_specs,
        out_specs=tok(d),
        out_shape=jax.ShapeDtypeStruct((n, d), F32),
        scratch_shapes=[pltpu.VMEM((tm, d), F32)],
        compiler_params=_params("parallel", "arbitrary", "arbitrary"),
        name="moe_ffn" if moe else "dense_ffn",
    )(*args)


def _ff_tile(ff):
    for t in (1408, 896, 512, 256, 128):
        if ff % t == 0 and ff // t >= 2:
            return t
    return ff


MOE_ROW_BLOCK = 144


def _moe_routed_kernel(nblk_ref, h_ref, x_ref, g2_ref, gate_ref, tri_ref, w1_ref, w3_ref, w2_ref, o_ref,
                       acc_scr, slotc_scr, slotr_scr, hsel_scr, ysel_scr, gsel_scr, *, rb, n_exp):
    i = pl.program_id(0)
    e = pl.program_id(1)
    f = pl.program_id(2)
    last_f = pl.num_programs(2) - 1
    tm = h_ref.shape[0]
    nb = nblk_ref[i * n_exp + e]
    dot = functools.partial(jnp.dot, preferred_element_type=F32)

    @pl.when((e == 0) & (f == 0))
    def _():
        acc_scr[...] = jnp.zeros(acc_scr.shape, F32)
        routed = gate_ref[...] > 0.0
        ind = jnp.where(routed, 1.0, 0.0).astype(BF16)
        tri = tri_ref[...]
        rank = dot(tri, ind)
        slotc_scr[...] = jnp.where(routed, rank, -1.0)
        eye = (_iota((LANES, LANES), 0) == _iota((LANES, LANES), 1)).astype(BF16)
        ind_t = _dot_nt(eye, ind)
        rank_t = _dot_nt(ind_t.astype(BF16), tri)
        slotr_scr[...] = jnp.where(ind_t > 0.5, rank_t, -1.0)

    @pl.when(f == 0)
    def _():
        slot_row = slotr_scr[pl.ds(e, 1), :]

        def select(b, carry):
            r0 = pl.multiple_of(b * rb, rb)
            want = (r0 + _iota((rb, 1), 0)).astype(F32)
            onehot = jnp.where(slot_row == want, 1.0, 0.0).astype(BF16)
            hsel_scr[pl.ds(r0, rb), :] = dot(onehot, h_ref[...]).astype(BF16)
            g = _dot_exact_lhs(onehot, gate_ref[...])
            ge = jnp.sum(jnp.where(_iota(g.shape, 1) == e, g, 0.0), axis=-1, keepdims=True)
            gsel_scr[pl.ds(r0, rb), :] = jnp.broadcast_to(ge, (rb, LANES))
            ysel_scr[pl.ds(r0, rb), :] = jnp.zeros((rb, ysel_scr.shape[1]), F32)
            return carry

        lax.fori_loop(0, nb, select, 0)

    def expert(b, carry):
        r0 = pl.multiple_of(b * rb, rb)
        hs = hsel_scr[pl.ds(r0, rb), :]
        a = dot(hs, w1_ref[0, 0])
        act = a * _sigmoid(a) * dot(hs, w3_ref[0, 0]) * gsel_scr[pl.ds(r0, rb), 0:1]
        ysel_scr[pl.ds(r0, rb), :] += dot(act.astype(BF16), w2_ref[0, 0])
        return carry

    lax.fori_loop(0, nb, expert, 0)

    @pl.when(f == last_f)
    def _():
        slotc = slotc_scr[...]
        slot_col = jnp.sum(jnp.where(_iota(slotc.shape, 1) == e, slotc, 0.0), axis=-1, keepdims=True)

        def scatter(b, carry):
            r0 = pl.multiple_of(b * rb, rb)
            want = (r0 + _iota((1, rb), 1)).astype(F32)
            onehot_t = jnp.where(slot_col == want, 1.0, 0.0).astype(BF16)
            acc_scr[...] += dot(onehot_t, ysel_scr[pl.ds(r0, rb), :].astype(BF16))
            return carry

        lax.fori_loop(0, nb, scatter, 0)

    @pl.when((e == n_exp - 1) & (f == last_f))
    def _():
        o_ref[...] = x_ref[...] + g2_ref[0] * acc_scr[...]


def _moe_routed_call(h2, x, rows, gate, w1, w3, w2, li, tf):
    n, d = x.shape
    _, n_exp, _, ff = w1.shape
    rb = MOE_ROW_BLOCK
    tm = _tile(n, 1024)
    n_tiles = n // tm
    sel_rows = -(-tm // rb) * rb
    counts = jnp.sum((gate[:, :n_exp] > 0.0).reshape(n_tiles, tm, n_exp).astype(jnp.int32), axis=1)
    nblk = ((counts + rb - 1) // rb).reshape(-1)
    t = jnp.arange(tm)
    tri = (t[None, :] < t[:, None]).astype(BF16)
    tok = lambda wd: pl.BlockSpec((tm, wd), lambda i, e, f, nb: (i, 0))
    mod = rows.mod_spec(tm, d, 5, 3)
    grid_spec = pltpu.PrefetchScalarGridSpec(
        num_scalar_prefetch=1,
        grid=(n_tiles, n_exp, ff // tf),
        in_specs=[tok(d), tok(d), pl.BlockSpec(mod.block_shape, lambda i, e, f, nb: mod.index_map(i)), tok(LANES),
                  pl.BlockSpec((tm, tm), lambda i, e, f, nb: (0, 0)),
                  pl.BlockSpec((1, 1, d, tf), lambda i, e, f, nb: (li, e, 0, f)),
                  pl.BlockSpec((1, 1, d, tf), lambda i, e, f, nb: (li, e, 0, f)),
                  pl.BlockSpec((1, 1, tf, d), lambda i, e, f, nb: (li, e, f, 0))],
        out_specs=tok(d),
        scratch_shapes=[pltpu.VMEM((tm, d), F32), pltpu.VMEM((tm, LANES), F32), pltpu.VMEM((LANES, tm), F32),
                        pltpu.VMEM((sel_rows, d), BF16), pltpu.VMEM((sel_rows, d), F32),
                        pltpu.VMEM((sel_rows, LANES), F32)],
    )
    return pl.pallas_call(
        functools.partial(_moe_routed_kernel, rb=rb, n_exp=n_exp),
        grid_spec=grid_spec,
        out_shape=jax.ShapeDtypeStruct((n, d), F32),
        compiler_params=_params("parallel", "arbitrary", "arbitrary"),
        name="moe_routed",
    )(nblk, h2, x, rows.mod3, gate, tri, w1, w3, w2)


def _block_ones(width, seg):
    i = jnp.arange(width) // seg
    return (i[:, None] == i[None, :]).astype(BF16)


def _rope_tables(pos, qk_dim, n_seg):
    half = qk_dim // 2
    inv_freq = ROPE_THETA ** (-jnp.arange(half, dtype=F32) / half)
    ang = pos.astype(F32)[:, None] * inv_freq[None, :]
    cos, sin = jnp.cos(ang), jnp.sin(ang)
    cos_t = jnp.tile(jnp.concatenate([cos, cos], axis=1), (1, n_seg))
    sin_t = jnp.tile(jnp.concatenate([-sin, sin], axis=1), (1, n_seg))
    return cos_t, sin_t


def kernel(x_prompt, x_sample, c_prompt, c_sample, cache_k, cache_v, state_wkv, state_shift, page_table,
           norm1_g, norm2_g, w_ada, b_ada, w_in, mu_shift, w0, w_up, a0, a_up, g_up, k_k, k_a, r_k,
           ln_x_g, ln_x_b, w_o_rwkv, q_norm_g, k_norm_g, lam, subln_g, w_o_diff, w_out,
           ffn_w1, ffn_w3, ffn_w2, router, moe_w1, moe_w3, moe_w2):
    n_batch, seq, d = x_prompt.shape
    n_dec, dec_seq, _ = x_sample.shape
    depth = w_in.shape[0]
    heads_r, hs = r_k.shape[1], r_k.shape[2]
    width_r = heads_r * hs
    proj_r = mu_shift.shape[-1]
    n_phys, page, heads_d, _, qk_dim = cache_k.shape[1:]
    v_dim = cache_v.shape[-1]
    qk_w = heads_d * 2 * qk_dim
    v_w = heads_d * v_dim
    w_lora, a_lora = w_up.shape[1], a_up.shape[1]
    assert v_dim == 2 * qk_dim == LANES and qk_w == v_w and w_lora + a_lora == LANES
    past_len = page_table.shape[1] * page
    n_p, n_s = n_batch * seq, n_dec * dec_seq

    o_q = proj_r
    o_k, o_v, o_g = o_q + qk_w, o_q + 2 * qk_w, o_q + 2 * qk_w + v_w
    w_in_f = jnp.concatenate([w_in[:, :, o_g:], w_in[:, :, o_q:o_g], w_in[:, :, :o_q]], axis=-1)
    w_in_b = w_in_f.astype(BF16)
    n_gate = w_in.shape[-1] - o_g
    assert n_gate == 2 * d and (n_gate + 3 * qk_w) % proj_r == 0 and n_gate % qk_w == 0
    gate_cols = (0, 1)
    qkv_cols = tuple(n_gate // qk_w + j for j in range(3))
    rwkv_col = (n_gate + 3 * qk_w) // proj_r
    tn = proj_r

    c_all = jnp.concatenate([c_prompt, c_sample], axis=0)
    c_all = jnp.pad(c_all, ((0, (-c_all.shape[0]) % 8), (0, 0)))
    mod = _ada_call(c_all, w_ada, b_ada)

    ones_head = _block_ones(width_r, hs)
    ones_qk = _block_ones(qk_w, qk_dim)
    cos_p, sin_p = _rope_tables(jnp.arange(seq), qk_dim, qk_w // qk_dim)
    cos_s, sin_s = _rope_tables(past_len + jnp.arange(dec_seq), qk_dim, qk_w // qk_dim)
    cos_s, sin_s = jnp.tile(cos_s, (n_dec, 1)), jnp.tile(sin_s, (n_dec, 1))
    cache_kt = jnp.transpose(cache_k, (0, 1, 3, 4, 5, 2))
    cache_v4 = cache_v.reshape(depth, n_phys, page * heads_d, v_dim)

    zpad = lambda m, before, after: jnp.pad(m, ((before, after), (0, 0)))
    bf = lambda a: a.astype(BF16)
    xp = x_prompt.reshape(n_p, d)
    xs = x_sample.reshape(n_s, d)
    shift0 = jnp.zeros((n_batch, 1, proj_r), F32)
    wkv_zero = jnp.zeros((n_batch, heads_r, hs, hs), F32)
    wkv_chunk = _tile(seq, 64)
    dec_chunk = max(16, dec_seq)
    outs = {key: [] for key in ("kp", "vp", "ks", "vs", "wp", "ws", "sp", "ss")}
    dense_f = (ffn_w1[:, None], ffn_w3[:, None], ffn_w2[:, None])
    moe_f = (moe_w1, moe_w3, moe_w2)
    dense_b = tuple(bf(w) for w in dense_f)
    moe_b = tuple(bf(w) for w in moe_f)

    for l in range(depth):
        lam_init = 0.8 - 0.6 * math.exp(-0.3 * l)
        lw = {
            "mu_shift": mu_shift[l][None], "w0": w0[l][None], "a0": a0[l][None],
            "w_up_pad": zpad(w_up[l], 0, a_lora), "a_up_pad": zpad(a_up[l], w_lora, 0), "g_up": g_up[l],
            "k_k": k_k[l][None], "k_a": k_a[l][None], "r_k": r_k[l].reshape(1, width_r),
            "ln_x_g": ln_x_g[l][None], "ln_x_b": ln_x_b[l][None],
            "subln_g": jnp.tile(subln_g[l], heads_d)[None], "norm2_g": norm2_g[l][None],
        }
        qg = jnp.tile(q_norm_g[l].reshape(-1), heads_d)[None]
        kg = jnp.tile(k_norm_g[l].reshape(-1), heads_d)[None]
        i = l // 2
        if l % 2 == 0:
            ffn_f, ffn_b = dense_f, dense_b
        else:
            n_experts = router.shape[-1]
            lw["router_pad"] = jnp.pad(router[i], ((0, 0), (0, LANES - n_experts)))
            lw["n_experts"] = n_experts
            ffn_f, ffn_b = moe_f, moe_b
        tf = _ff_tile(ffn_f[0].shape[-1])
        norm1 = norm1_g[l][None]
        out_w = {"w_o_rwkv": w_o_rwkv[l], "w_o_diff": w_o_diff[l], "w_out": w_out[l]}
        lw_p = dict(lw, **{key: bf(val) for key, val in out_w.items()})
        lw_s = dict(lw, **out_w)

        def run_group(x, rows, init_rows, state0, is_prompt):
            lw_g, w_in_g, ffn = (lw_p, w_in_b[l], ffn_b) if is_prompt else (lw_s, w_in_f[l], ffn_f)
            proj = _proj_call(x, rows, norm1, w_in_g, tn)
            r, ld, k2, v_r, av, bv, g = _rwkv_prep_call(proj, rows, init_rows, lw_g, rwkv_col, ones_head,
                                                        not is_prompt)
            if is_prompt:
                y, state = _wkv_call(r, ld, k2, v_r, av, bv, state0, rows.seq_len, wkv_chunk, False,
                                     n_sub=WKV_CHUNKS_PER_STEP if seq % (wkv_chunk * WKV_CHUNKS_PER_STEP) == 0 else 1)
                k_rows, v_rows, q_bf, k_bf, v_aug = _attn_prep_call(
                    proj, rows, cos_p, sin_p, qg, kg, ones_qk, qkv_cols, BF16, qk_dim)
                o = _flash_call(q_bf, k_bf, v_aug, lam[l], n_batch, seq, heads_d, qk_dim, lam_init)
            else:
                padt = lambda a: jnp.pad(a.reshape(n_dec, dec_seq, width_r),
                                         ((0, 0), (0, dec_chunk - dec_seq), (0, 0))).reshape(-1, width_r)
                y, state = _wkv_call(padt(r), padt(ld), padt(k2), padt(v_r), padt(av), padt(bv), state0,
                                     dec_chunk, dec_chunk, True)
                y = y.reshape(n_dec, dec_chunk, width_r)[:, :dec_seq].reshape(-1, width_r)
                k_rows, v_rows, q_f, _, _ = _attn_prep_call(
                    proj, rows, cos_s, sin_s, qg, kg, ones_qk, qkv_cols, F32, qk_dim)
                o = _paged_call(page_table, lam[l], q_f, k_rows, v_rows, cache_kt, cache_v4, l, heads_d,
                                qk_dim, lam_init, dec_seq)
            post = _post_call(y, r, k2, v_r, g, o, proj, x, rows, lw_g, ones_head, gate_cols, lam_init, hs, v_dim)
            x_new, h2 = post[0], post[1]
            gate = post[2] if len(post) > 2 else None
            ffn_call = _moe_routed_call if (is_prompt and gate is not None) else _ffn_call
            x_out = ffn_call(h2, x_new, rows, gate, *ffn, i, tf)
            shift = proj.reshape(-1, rows.seq_len, proj.shape[-1])[:, -1, rwkv_col * proj_r:]
            return x_out, k_rows, v_rows, state, shift

        rows_p = _Rows(n_p, seq, mod[l, :n_batch][:, None, :])
        rows_s = _Rows(n_s, dec_seq, jnp.repeat(mod[l, n_batch:n_batch + n_dec], dec_seq, axis=0)[None])
        xp, kr, vr, st, sh = run_group(xp, rows_p, shift0, wkv_zero, True)
        outs["kp"].append(kr.reshape(n_batch, seq, heads_d, 2, qk_dim))
        outs["vp"].append(vr.reshape(n_batch, seq, heads_d, v_dim))
        outs["wp"].append(st)
        outs["sp"].append(sh)
        init_s = jnp.repeat(state_shift[l], dec_seq, axis=0)
        xs, kr, vr, st, sh = run_group(xs, rows_s, init_s, state_wkv[l], False)
        outs["ks"].append(kr.reshape(n_dec, dec_seq, heads_d, 2, qk_dim))
        outs["vs"].append(vr.reshape(n_dec, dec_seq, heads_d, v_dim))
        outs["ws"].append(st)
        outs["ss"].append(sh)

    st = lambda key: jnp.stack(outs[key])
    return (xp.reshape(n_batch, seq, d), xs.reshape(n_dec, dec_seq, d), st("kp"), st("vp"), st("ks"), st("vs"),
            st("wp"), st("ws"), st("sp"), st("ss"))
```
